```python
import jax
import jax.numpy as jnp
from jax import lax
import numpy as np

D_MODEL = 1024
BATCH = 2
SEQ = 8192
DEPTH = 1
DEC_BATCH = 128
DEC_SEQ = 8
PAST_LEN = 8192
PAGE_SIZE = 128

MIX_WIDTH = D_MODEL
POOL_WIDTH = MIX_WIDTH // 2
POOL_WINDOWS = (2, 4, 8, 16)
POOL_GROUPS = len(POOL_WINDOWS)
POOL_GW = POOL_WIDTH // POOL_GROUPS
POOL_STATE = max(POOL_WINDOWS) - 1
ATT_WIDTH = MIX_WIDTH - POOL_WIDTH
N_HEADS = 8
HEAD_DIM = ATT_WIDTH // N_HEADS
ROT_DIM = HEAD_DIM // 4
ROPE_THETA = 500000.0
MOBA_BLOCK = 256
MOBA_TOPK = 3
Q_BLOCK = 128
MEM_LEN = 256
MEM_HEADS = 4
MEM_HEAD_DIM = D_MODEL // MEM_HEADS
D_FF = 4 * D_MODEL
IN_WIDTH = POOL_WIDTH + 3 * ATT_WIDTH
EPS = 1e-6

kernel_name = 'moba_pool_hybrid_step'


def _rmsnorm(x, g):
    xf = x.astype(jnp.float32)
    y = xf * lax.rsqrt(jnp.mean(xf * xf, axis=-1, keepdims=True) + EPS)
    return (y * g.astype(jnp.float32)).astype(x.dtype)


def _rope(x, pos):
    half = ROT_DIM // 2
    inv = ROPE_THETA ** (-2.0 * jnp.arange(half, dtype=jnp.float32) / ROT_DIM)
    ang = pos[:, None] * inv[None, :]
    cos = jnp.cos(ang)[:, None, :]
    sin = jnp.sin(ang)[:, None, :]
    xf = x.astype(jnp.float32)
    x1 = xf[..., :half]
    x2 = xf[..., half:ROT_DIM]
    out = jnp.concatenate([x1 * cos - x2 * sin, x2 * cos + x1 * sin, xf[..., ROT_DIM:]], axis=-1)
    return out.astype(x.dtype)


def _in_proj(h, g_pre, w_in, pos):
    n = _rmsnorm(h, g_pre)
    z = n @ w_in
    B, S = h.shape[0], h.shape[1]
    u = z[..., :POOL_WIDTH]
    q = z[..., POOL_WIDTH:POOL_WIDTH + ATT_WIDTH].reshape(B, S, N_HEADS, HEAD_DIM)
    k = z[..., POOL_WIDTH + ATT_WIDTH:POOL_WIDTH + 2 * ATT_WIDTH].reshape(B, S, N_HEADS, HEAD_DIM)
    v = z[..., POOL_WIDTH + 2 * ATT_WIDTH:].reshape(B, S, N_HEADS, HEAD_DIM)
    return u, _rope(q, pos), _rope(k, pos), v


def _pool_mix(u_ext, pos0, w_pool, pool_scale):
    P = POOL_STATE
    B, T = u_ext.shape[0], u_ext.shape[1] - P
    uf = u_ext.astype(jnp.float32)
    cs = jnp.pad(jnp.cumsum(uf, axis=1), ((0, 0), (1, 0), (0, 0)))
    pos = pos0 + jnp.arange(T)
    pooled = []
    for g, w in enumerate(POOL_WINDOWS):
        sl = slice(g * POOL_GW, (g + 1) * POOL_GW)
        s = cs[:, P + 1:P + T + 1, sl] - cs[:, P + 1 - w:P + T + 1 - w, sl]
        cnt = jnp.minimum(pos + 1, w).astype(jnp.float32)[None, :, None]
        pooled.append(s / cnt)
    pooled = jnp.stack(pooled, axis=2)
    d = pooled - uf[:, P:].reshape(B, T, POOL_GROUPS, POOL_GW)
    y = jnp.einsum('btgc,gcd->btgd', d, w_pool.astype(jnp.float32)).reshape(B, T, POOL_WIDTH)
    return (y * pool_scale.astype(jnp.float32)).astype(u_ext.dtype)


def _moba_attend(q, k_own, v_own, own_mask, k_sel, v_sel, sel_valid):
    scale = HEAD_DIM ** -0.5
    s_own = jnp.einsum('bhqd,bhkd->bhqk', q, k_own, preferred_element_type=jnp.float32) * scale
    s_own = jnp.where(own_mask, s_own, -jnp.inf)
    if k_sel is None:
        p = jax.nn.softmax(s_own, axis=-1).astype(v_own.dtype)
        return jnp.einsum('bhqk,bhkd->bhqd', p, v_own)
    s_sel = jnp.einsum('bhqd,bhqsld->bhqsl', q, k_sel, preferred_element_type=jnp.float32) * scale
    if sel_valid is not None:
        s_sel = jnp.where(sel_valid[..., None], s_sel, -jnp.inf)
    n_sel = s_sel.shape[3] * s_sel.shape[4]
    s = jnp.concatenate([s_sel.reshape(s_sel.shape[:3] + (n_sel,)), s_own], axis=-1)
    p = jax.nn.softmax(s, axis=-1).astype(v_own.dtype)
    p_sel = p[..., :n_sel].reshape(s_sel.shape)
    p_own = p[..., n_sel:]
    return jnp.einsum('bhqsl,bhqsld->bhqd', p_sel, v_sel) + jnp.einsum('bhqk,bhkd->bhqd', p_own, v_own)


def _moba_prompt(q, k, v):
    B, S = q.shape[0], q.shape[1]
    nb = -(-S // MOBA_BLOCK)
    pad = nb * MOBA_BLOCK - S
    kh = jnp.pad(k, ((0, 0), (0, pad), (0, 0), (0, 0))).transpose(0, 2, 1, 3).reshape(B, N_HEADS, nb, MOBA_BLOCK, HEAD_DIM)
    vh = jnp.pad(v, ((0, 0), (0, pad), (0, 0), (0, 0))).transpose(0, 2, 1, 3).reshape(B, N_HEADS, nb, MOBA_BLOCK, HEAD_DIM)
    kmean = jnp.mean(kh.astype(jnp.float32), axis=3)
    n_sel = min(MOBA_TOPK, nb)
    nq = S // Q_BLOCK
    qc = q.transpose(0, 2, 1, 3).reshape(B, N_HEADS, nq, Q_BLOCK, HEAD_DIM).transpose(2, 0, 1, 3, 4)
    bi = jnp.arange(B)[:, None, None, None]
    hi = jnp.arange(N_HEADS)[None, :, None, None]

    def step(args):
        c, qb = args
        qpos = c * Q_BLOCK + jnp.arange(Q_BLOCK)
        blk = (c * Q_BLOCK) // MOBA_BLOCK
        k_own = lax.dynamic_index_in_dim(kh, blk, axis=2, keepdims=False)
        v_own = lax.dynamic_index_in_dim(vh, blk, axis=2, keepdims=False)
        kpos = blk * MOBA_BLOCK + jnp.arange(MOBA_BLOCK)
        own_mask = kpos[None, :] <= qpos[:, None]
        gate = jnp.einsum('bhqd,bhnd->bhqn', qb.astype(jnp.float32), kmean)
        gate = jnp.where(jnp.arange(nb) < blk, gate, -jnp.inf)
        _, idx = lax.top_k(gate, n_sel)
        valid = idx < blk
        k_sel = kh[bi, hi, idx]
        v_sel = vh[bi, hi, idx]
        return _moba_attend(qb, k_own, v_own, own_mask, k_sel, v_sel, valid)

    o = lax.map(step, (jnp.arange(nq), qc))
    return o.transpose(1, 0, 3, 2, 4).reshape(B, S, ATT_WIDTH)


def _moba_sample(q, k_new, v_new, cache_k, cache_v, page_table):
    DB, T = q.shape[0], q.shape[1]
    ppb = MOBA_BLOCK // PAGE_SIZE
    own_blk = PAST_LEN // MOBA_BLOCK
    own_start = own_blk * MOBA_BLOCK
    n_own_pages = (PAST_LEN - own_start) // PAGE_SIZE
    p0 = own_start // PAGE_SIZE
    own_pages = page_table[:, p0:p0 + n_own_pages]
    k_past = cache_k[own_pages].reshape(DB, n_own_pages * PAGE_SIZE, N_HEADS, HEAD_DIM)
    v_past = cache_v[own_pages].reshape(DB, n_own_pages * PAGE_SIZE, N_HEADS, HEAD_DIM)
    k_own = jnp.concatenate([k_past.astype(k_new.dtype), k_new], axis=1).transpose(0, 2, 1, 3)
    v_own = jnp.concatenate([v_past.astype(v_new.dtype), v_new], axis=1).transpose(0, 2, 1, 3)
    kpos = own_start + jnp.arange(k_own.shape[2])
    qpos = PAST_LEN + jnp.arange(T)
    own_mask = kpos[None, :] <= qpos[:, None]
    qh = q.transpose(0, 2, 1, 3)
    n_sel = min(MOBA_TOPK, own_blk)
    if n_sel == 0:
        o = _moba_attend(qh, k_own, v_own, own_mask, None, None, None)
        return o.transpose(0, 2, 1, 3).reshape(DB, T, ATT_WIDTH)
    page_sum = jnp.sum(cache_k, axis=1, dtype=jnp.float32)
    blk_sum = page_sum[page_table[:, :own_blk * ppb]].reshape(DB, own_blk, ppb, N_HEADS, HEAD_DIM).sum(axis=2)
    kmean = (blk_sum / MOBA_BLOCK).transpose(0, 2, 1, 3)
    gate = jnp.einsum('bhtd,bhnd->bhtn', qh.astype(jnp.float32), kmean)
    _, idx = lax.top_k(gate, n_sel)
    logical = idx[..., None] * ppb + jnp.arange(ppb)
    phys = page_table[jnp.arange(DB)[:, None, None, None, None], logical]
    hi = jnp.arange(N_HEADS)[None, :, None, None]

    def step(args):
        qt, pt, mt = args
        ks = cache_k[pt, :, hi, :].reshape(DB, N_HEADS, 1, n_sel, MOBA_BLOCK, HEAD_DIM).astype(qt.dtype)
        vs = cache_v[pt, :, hi, :].reshape(DB, N_HEADS, 1, n_sel, MOBA_BLOCK, HEAD_DIM).astype(qt.dtype)
        return _moba_attend(qt[:, :, None], k_own, v_own, mt[None], ks, vs, None)[:, :, 0]

    o = lax.map(step, (qh.transpose(2, 0, 1, 3), phys.transpose(2, 0, 1, 3, 4), own_mask))
    return o.transpose(1, 0, 2, 3).reshape(DB, T, ATT_WIDTH)


def _mem_kv(mem, g_mem, w_xk, w_xv):
    B = mem.shape[0]
    mn = _rmsnorm(mem, g_mem)
    mk = (mn @ w_xk).reshape(B, MEM_LEN, MEM_HEADS, MEM_HEAD_DIM)
    mv = (mn @ w_xv).reshape(B, MEM_LEN, MEM_HEADS, MEM_HEAD_DIM)
    return mk, mv


def _cross_attn(h, mk, mv, g_pre, w_xq, w_xo, g_post):
    B, S = h.shape[0], h.shape[1]
    n = _rmsnorm(h, g_pre)
    q = (n @ w_xq).reshape(B, S, MEM_HEADS, MEM_HEAD_DIM)
    s = jnp.einsum('bshd,bmhd->bhsm', q, mk.astype(q.dtype), preferred_element_type=jnp.float32) * (MEM_HEAD_DIM ** -0.5)
    p = jax.nn.softmax(s, axis=-1).astype(h.dtype)
    o = jnp.einsum('bhsm,bmhd->bshd', p, mv.astype(h.dtype)).reshape(B, S, D_MODEL)
    return h + _rmsnorm(o @ w_xo, g_post)


def _ffn(h, g_pre, w_up, w_down, g_post):
    n = _rmsnorm(h, g_pre)
    a = jnp.square(jax.nn.relu(n @ w_up))
    return h + _rmsnorm(a @ w_down, g_post)


def setup_inputs(seed: int = 0) -> dict:
    key = jax.random.key(seed)
    ks = jax.random.split(key, 32)
    f32 = jnp.float32
    n_pages = PAST_LEN // PAGE_SIZE
    n_phys = (DEC_BATCH * n_pages * 5) // 4

    def nrm(k, shape, scale=1.0):
        return jax.random.normal(k, shape, f32) * scale

    def gain(k, shape):
        return 1.0 + 0.05 * jax.random.normal(k, shape, f32)

    page_table = jax.random.permutation(ks[8], n_phys)[:DEC_BATCH * n_pages].reshape(DEC_BATCH, n_pages).astype(jnp.int32)
    return {
        'x_prompt': nrm(ks[0], (BATCH, SEQ, D_MODEL)),
        'x_sample': nrm(ks[1], (DEC_BATCH, DEC_SEQ, D_MODEL)),
        'mem_prompt': nrm(ks[2], (BATCH, MEM_LEN, D_MODEL)),
        'cache_k': nrm(ks[3], (DEPTH, n_phys, PAGE_SIZE, N_HEADS, HEAD_DIM)),
        'cache_v': nrm(ks[4], (DEPTH, n_phys, PAGE_SIZE, N_HEADS, HEAD_DIM)),
        'cache_mem_k': nrm(ks[5], (DEPTH, DEC_BATCH, MEM_LEN, MEM_HEADS, MEM_HEAD_DIM)),
        'cache_mem_v': nrm(ks[6], (DEPTH, DEC_BATCH, MEM_LEN, MEM_HEADS, MEM_HEAD_DIM)),
        'state_pool': nrm(ks[7], (DEPTH, DEC_BATCH, POOL_STATE, POOL_WIDTH)),
        'page_table': page_table,
        'g_pre_mix': gain(ks[9], (DEPTH, D_MODEL)),
        'w_in': nrm(ks[10], (DEPTH, D_MODEL, IN_WIDTH), D_MODEL ** -0.5),
        'w_pool': nrm(ks[11], (DEPTH, POOL_GROUPS, POOL_GW, POOL_GW), POOL_GW ** -0.5),
        'pool_scale': 1.0 + 0.1 * jax.random.normal(ks[12], (DEPTH, POOL_WIDTH), f32),
        'w_out': nrm(ks[13], (DEPTH, MIX_WIDTH, D_MODEL), MIX_WIDTH ** -0.5),
        'g_post_mix': gain(ks[14], (DEPTH, D_MODEL)),
        'g_mem': gain(ks[15], (DEPTH, D_MODEL)),
        'g_pre_x': gain(ks[16], (DEPTH, D_MODEL)),
        'w_xq': nrm(ks[17], (DEPTH, D_MODEL, D_MODEL), D_MODEL ** -0.5),
        'w_xk': nrm(ks[18], (DEPTH, D_MODEL, D_MODEL), D_MODEL ** -0.5),
        'w_xv': nrm(ks[19], (DEPTH, D_MODEL, D_MODEL), D_MODEL ** -0.5),
        'w_xo': nrm(ks[20], (DEPTH, D_MODEL, D_MODEL), D_MODEL ** -0.5),
        'g_post_x': gain(ks[21], (DEPTH, D_MODEL)),
        'g_pre_ffn': gain(ks[22], (DEPTH, D_MODEL)),
        'w_up': nrm(ks[23], (DEPTH, D_MODEL, D_FF), D_MODEL ** -0.5),
        'w_down': nrm(ks[24], (DEPTH, D_FF, D_MODEL), D_FF ** -0.5),
        'g_post_ffn': gain(ks[25], (DEPTH, D_MODEL)),
    }


def reference(x_prompt, x_sample, mem_prompt, cache_k, cache_v, cache_mem_k, cache_mem_v, state_pool, page_table,
              g_pre_mix, w_in, w_pool, pool_scale, w_out, g_post_mix,
              g_mem, g_pre_x, w_xq, w_xk, w_xv, w_xo, g_post_x,
              g_pre_ffn, w_up, w_down, g_post_ffn):
    pos_p = jnp.arange(SEQ, dtype=jnp.float32)
    pos_s = PAST_LEN + jnp.arange(DEC_SEQ, dtype=jnp.float32)
    hp, hs = x_prompt, x_sample
    pool_p, k_p, v_p, mk_p, mv_p = [], [], [], [], []
    pool_s, k_s, v_s = [], [], []
    for l in range(DEPTH):
        u, q, k, v = _in_proj(hp, g_pre_mix[l], w_in[l], pos_p)
        u_ext = jnp.pad(u, ((0, 0), (POOL_STATE, 0), (0, 0)))
        mix = jnp.concatenate([_pool_mix(u_ext, 0, w_pool[l], pool_scale[l]), _moba_prompt(q, k, v)], axis=-1)
        hp = hp + _rmsnorm(mix @ w_out[l], g_post_mix[l])
        mk, mv = _mem_kv(mem_prompt, g_mem[l], w_xk[l], w_xv[l])
        hp = _cross_attn(hp, mk, mv, g_pre_x[l], w_xq[l], w_xo[l], g_post_x[l])
        hp = _ffn(hp, g_pre_ffn[l], w_up[l], w_down[l], g_post_ffn[l])
        pool_p.append(u[:, SEQ - POOL_STATE:])
        k_p.append(k)
        v_p.append(v)
        mk_p.append(mk)
        mv_p.append(mv)
        u, q, k, v = _in_proj(hs, g_pre_mix[l], w_in[l], pos_s)
        u_ext = jnp.concatenate([state_pool[l].astype(u.dtype), u], axis=1)
        mix = jnp.concatenate([_pool_mix(u_ext, PAST_LEN, w_pool[l], pool_scale[l]),
                               _moba_sample(q, k, v, cache_k[l], cache_v[l], page_table)], axis=-1)
        hs = hs + _rmsnorm(mix @ w_out[l], g_post_mix[l])
        hs = _cross_attn(hs, cache_mem_k[l], cache_mem_v[l], g_pre_x[l], w_xq[l], w_xo[l], g_post_x[l])
        hs = _ffn(hs, g_pre_ffn[l], w_up[l], w_down[l], g_post_ffn[l])
        pool_s.append(u_ext[:, u_ext.shape[1] - POOL_STATE:])
        k_s.append(k)
        v_s.append(v)
    return (hp, hs,
            jnp.stack(pool_p), jnp.stack(k_p), jnp.stack(v_p), jnp.stack(mk_p), jnp.stack(mv_p),
            jnp.stack(pool_s), jnp.stack(k_s), jnp.stack(v_s))
```

```python
import functools

import numpy as np
import jax
import jax.numpy as jnp
from jax import lax
from jax.experimental import pallas as pl
from jax.experimental.pallas import tpu as pltpu

F32 = jnp.float32
BF16 = jnp.bfloat16

D_MODEL = 1024
SEQ = 8192
DEC_SEQ = 8
PAST_LEN = 8192
PAGE_SIZE = 128
POOL_WIDTH = 512
POOL_WINDOWS = (2, 4, 8, 16)
POOL_GW = 128
POOL_STATE = 15
ATT_WIDTH = 512
N_HEADS = 8
HEAD_DIM = 64
ROT_DIM = 16
ROPE_THETA = 500000.0
MOBA_BLOCK = 256
MOBA_TOPK = 3
MEM_LEN = 256
MEM_HEADS = 4
MEM_HEAD_DIM = 256
D_FF = 4096
EPS = 1e-6

LANES = 128
SUBLANES = 8

ROW_TILE = 512
POOL_HALO = 16
N_PAST_BLOCKS = PAST_LEN // MOBA_BLOCK
PAGES_PER_BLOCK = MOBA_BLOCK // PAGE_SIZE
HEADS_PER_VREG = LANES // HEAD_DIM
GATE_PAD = LANES
KV_SLOTS = 8
ROT_HALF = ROT_DIM // 2
NEG_INF = float("-inf")

assert PAST_LEN % MOBA_BLOCK == 0, "sample own-block is assumed to hold new keys only"
assert PAST_LEN + 1 >= max(POOL_WINDOWS)
assert ROT_HALF == SUBLANES, "feature-major rotary assumes one sublane group per rotary half"


def _params(n_grid, vmem_mib):
    return pltpu.CompilerParams(
        dimension_semantics=("arbitrary",) * n_grid,
        vmem_limit_bytes=vmem_mib * 1024 * 1024,
    )


def _rmsnorm(x, g):
    ms = jnp.mean(x * x, axis=-1, keepdims=True)
    return x * lax.rsqrt(ms + EPS) * g


def _dot(a, b):
    return jnp.dot(a, b, preferred_element_type=F32)


def _dot_nt(a, b, precision=None):
    return lax.dot_general(a, b, (((1,), (1,)), ((), ())), precision=precision,
                           preferred_element_type=F32)


def _topk_mask(gate, valid):
    n = gate.shape[1]
    idx = lax.broadcasted_iota(jnp.int32, gate.shape, 1).astype(F32)
    g = jnp.where(valid, gate, NEG_INF)
    sel = jnp.zeros(gate.shape, F32)
    for _ in range(MOBA_TOPK):
        mx = jnp.max(g, axis=1, keepdims=True)
        first = jnp.min(jnp.where(g == mx, idx, float(n)), axis=1, keepdims=True)
        pick = idx == first
        sel = jnp.where(pick, 1.0, sel)
        g = jnp.where(pick, NEG_INF, g)
    return jnp.where(valid, sel, 0.0)


def _rope_rows(z, pos_rows, inv_lanes):
    tm = z.shape[0]
    ang = pos_rows * inv_lanes
    cos = jnp.cos(ang)
    sin = jnp.sin(ang)
    d = lax.rem(lax.broadcasted_iota(jnp.int32, (tm, LANES), 1), HEAD_DIM)
    sin_lo = jnp.where(d < ROT_HALF, -sin, 0.0)
    sin_hi = jnp.where((d >= ROT_HALF) & (d < ROT_DIM), sin, 0.0)
    out = []
    for c in range(ATT_WIDTH // LANES):
        zz = z[:, c * LANES:(c + 1) * LANES]
        out.append(zz * cos
                   + pltpu.roll(zz, LANES - ROT_HALF, 1) * sin_lo
                   + pltpu.roll(zz, ROT_HALF, 1) * sin_hi)
    return out


def _in_proj_rows_kernel(x_ref, g_ref, w_ref, inv_ref, u_ref, q_ref, k_ref, v_ref, *, period, offset):
    tm = x_ref.shape[0]
    i = pl.program_id(0)
    n = _rmsnorm(x_ref[...], g_ref[...]).astype(BF16)

    def seg(j):
        return _dot(n, w_ref[:, j * ATT_WIDTH:(j + 1) * ATT_WIDTH])

    u_ref[...] = seg(0)
    v_ref[...] = seg(3)
    row = lax.broadcasted_iota(jnp.int32, (tm, LANES), 0) + i * tm
    pos = (lax.rem(row, period) + offset).astype(F32)
    for j, ref in ((1, q_ref), (2, k_ref)):
        for c, piece in enumerate(_rope_rows(seg(j), pos, inv_ref[...])):
            ref[:, c * LANES:(c + 1) * LANES] = piece


def _in_proj_rows(x, g, w_bf, inv_lanes, period, offset):
    m = x.shape[0]
    tm = min(ROW_TILE, m)
    out = jax.ShapeDtypeStruct((m, ATT_WIDTH), F32)
    row_spec = pl.BlockSpec((tm, ATT_WIDTH), lambda i: (i, 0))
    return pl.pallas_call(
        functools.partial(_in_proj_rows_kernel, period=period, offset=offset),
        grid=(m // tm,),
        in_specs=[
            pl.BlockSpec((tm, D_MODEL), lambda i: (i, 0)),
            pl.BlockSpec((1, D_MODEL), lambda i: (0, 0)),
            pl.BlockSpec((D_MODEL, 4 * ATT_WIDTH), lambda i: (0, 0)),
            pl.BlockSpec((1, LANES), lambda i: (0, 0)),
        ],
        out_specs=[row_spec] * 4,
        out_shape=[out] * 4,
        compiler_params=_params(1, 40),
        name="in_proj_rows",
    )(x, g, w_bf, inv_lanes)


def _in_proj_prompt_kernel(x_ref, g_ref, wuq_ref, wkvt_ref, inv_ref, invc_ref, u_ref, q_ref, kt_ref, vt_ref):
    tm = x_ref.shape[0]
    i = pl.program_id(1)
    n = _rmsnorm(x_ref[...], g_ref[...]).astype(BF16)
    u_ref[...] = _dot(n, wuq_ref[:, 0:POOL_WIDTH])
    row = lax.broadcasted_iota(jnp.int32, (tm, LANES), 0) + i * tm
    q = _dot(n, wuq_ref[:, POOL_WIDTH:])
    for c, piece in enumerate(_rope_rows(q, row.astype(F32), inv_ref[...])):
        q_ref[:, c * LANES:(c + 1) * LANES] = piece

    vt_ref[0] = _dot_nt(wkvt_ref[ATT_WIDTH:, :], n)
    kt = _dot_nt(wkvt_ref[0:ATT_WIDTH, :], n)
    pos_t = (lax.broadcasted_iota(jnp.int32, (ROT_HALF, tm), 1) + i * tm).astype(F32)
    ang = invc_ref[...] * pos_t
    cos = jnp.cos(ang)
    sin = jnp.sin(ang)
    for h in range(N_HEADS):
        r0 = h * HEAD_DIM
        x1 = kt[r0:r0 + ROT_HALF, :]
        x2 = kt[r0 + ROT_HALF:r0 + ROT_DIM, :]
        kt_ref[0, r0:r0 + ROT_HALF, :] = x1 * cos - x2 * sin
        kt_ref[0, r0 + ROT_HALF:r0 + ROT_DIM, :] = x2 * cos + x1 * sin
        kt_ref[0, r0 + ROT_DIM:r0 + HEAD_DIM, :] = kt[r0 + ROT_DIM:r0 + HEAD_DIM, :]


def _in_proj_prompt(x, g, wuq_bf, wkvt_bf, inv_lanes, inv_col, batch):
    tm = ROW_TILE
    tiles = SEQ // tm
    row_spec = pl.BlockSpec((tm, ATT_WIDTH), lambda b, i: (b * tiles + i, 0))
    t_spec = pl.BlockSpec((1, ATT_WIDTH, tm), lambda b, i: (b, 0, i))
    rows = jax.ShapeDtypeStruct((batch * SEQ, ATT_WIDTH), F32)
    feat = jax.ShapeDtypeStruct((batch, ATT_WIDTH, SEQ), F32)
    return pl.pallas_call(
        _in_proj_prompt_kernel,
        grid=(batch, tiles),
        in_specs=[
            pl.BlockSpec((tm, D_MODEL), lambda b, i: (b * tiles + i, 0)),
            pl.BlockSpec((1, D_MODEL), lambda b, i: (0, 0)),
            pl.BlockSpec((D_MODEL, POOL_WIDTH + ATT_WIDTH), lambda b, i: (0, 0)),
            pl.BlockSpec((2 * ATT_WIDTH, D_MODEL), lambda b, i: (0, 0)),
            pl.BlockSpec((1, LANES), lambda b, i: (0, 0)),
            pl.BlockSpec((ROT_HALF, 1), lambda b, i: (0, 0)),
        ],
        out_specs=[row_spec, row_spec, t_spec, t_spec],
        out_shape=[rows, rows, feat, feat],
        compiler_params=_params(2, 40),
        name="in_proj_prompt",
    )(x, g, wuq_bf, wkvt_bf, inv_lanes, inv_col)


def _pool_prompt_kernel(u_ref, prev_ref, w_ref, sc_ref, o_ref, ext_ref):
    tm = u_ref.shape[0]
    i = pl.program_id(0)
    pos0 = lax.rem(i * tm, SEQ)
    ext_ref[0:POOL_HALO, :] = jnp.where(pos0 == 0, 0.0, prev_ref[...])
    ext_ref[POOL_HALO:, :] = u_ref[...]
    pos = pos0 + lax.broadcasted_iota(jnp.int32, (tm, POOL_GW), 0)
    for g, w in enumerate(POOL_WINDOWS):
        cols = slice(g * POOL_GW, (g + 1) * POOL_GW)
        cur = u_ref[:, cols]
        s = cur
        for k in range(1, w):
            s = s + ext_ref[POOL_HALO - k:POOL_HALO - k + tm, cols]
        cnt = jnp.minimum(pos + 1, w).astype(F32)
        d = s / cnt - cur
        o_ref[:, cols] = _dot(d.astype(BF16), w_ref[g]) * sc_ref[:, cols]


def _pool_prompt(u, w_pool_bf, pool_scale):
    m = u.shape[0]
    tm = ROW_TILE
    halo_per_tile = tm // POOL_HALO
    return pl.pallas_call(
        _pool_prompt_kernel,
        grid=(m // tm,),
        in_specs=[
            pl.BlockSpec((tm, POOL_WIDTH), lambda i: (i, 0)),
            pl.BlockSpec((POOL_HALO, POOL_WIDTH), lambda i: (jnp.maximum(i * halo_per_tile - 1, 0), 0)),
            pl.BlockSpec((len(POOL_WINDOWS), POOL_GW, POOL_GW), lambda i: (0, 0, 0)),
            pl.BlockSpec((1, POOL_WIDTH), lambda i: (0, 0)),
        ],
        out_specs=pl.BlockSpec((tm, POOL_WIDTH), lambda i: (i, 0)),
        out_shape=jax.ShapeDtypeStruct((m, POOL_WIDTH), F32),
        scratch_shapes=[pltpu.VMEM((POOL_HALO + tm, POOL_WIDTH), F32)],
        compiler_params=_params(1, 32),
        name="pool_prompt",
    )(u, u, w_pool_bf, pool_scale)


def _pool_sample_kernel(e_ref, w_ref, sc_ref, o_ref):
    nb = e_ref.shape[0]
    lo = 1 + POOL_STATE
    t = lax.broadcasted_iota(jnp.int32, (nb, DEC_SEQ, POOL_GW), 1)
    for g, w in enumerate(POOL_WINDOWS):
        cols = slice(g * POOL_GW, (g + 1) * POOL_GW)
        cur = e_ref[:, lo:lo + DEC_SEQ, cols]
        s = cur
        for k in range(1, w):
            s = s + e_ref[:, lo - k:lo - k + DEC_SEQ, cols]
        cnt = jnp.minimum(PAST_LEN + t + 1, w).astype(F32)
        d = (s / cnt - cur).reshape(nb * DEC_SEQ, POOL_GW)
        o_ref[:, cols] = _dot(d.astype(BF16), w_ref[g]) * sc_ref[:, cols]


def _pool_sample(ext, w_pool_bf, pool_scale):
    db, rows, _ = ext.shape
    nb = 32
    return pl.pallas_call(
        _pool_sample_kernel,
        grid=(db // nb,),
        in_specs=[
            pl.BlockSpec((nb, rows, POOL_WIDTH), lambda i: (i, 0, 0)),
            pl.BlockSpec((len(POOL_WINDOWS), POOL_GW, POOL_GW), lambda i: (0, 0, 0)),
            pl.BlockSpec((1, POOL_WIDTH), lambda i: (0, 0)),
        ],
        out_specs=pl.BlockSpec((nb * DEC_SEQ, POOL_WIDTH), lambda i: (i, 0)),
        out_shape=jax.ShapeDtypeStruct((db * DEC_SEQ, POOL_WIDTH), F32),
        compiler_params=_params(1, 32),
        name="pool_sample",
    )(ext, w_pool_bf, pool_scale)


def _moba_prompt_kernel(q_ref, kt_ref, vt_ref, o_ref, kmean_ref):
    tq = MOBA_BLOCK
    blk = pl.program_id(2)
    n_blocks = SEQ // MOBA_BLOCK
    scale = HEAD_DIM ** -0.5

    @pl.when(blk == 0)
    def _():
        col = lax.broadcasted_iota(jnp.int32, (LANES, GATE_PAD), 1)
        km = jnp.zeros((LANES, GATE_PAD), F32)
        for n in range(n_blocks):
            kb = kt_ref[0, :, n * MOBA_BLOCK:(n + 1) * MOBA_BLOCK]
            km = jnp.where(col == n, jnp.sum(kb, axis=1, keepdims=True) * (1.0 / MOBA_BLOCK), km)
        kmean_ref[...] = km

    q2 = q_ref[0]
    lane = lax.broadcasted_iota(jnp.int32, (tq, LANES), 1)
    gate_idx = lax.broadcasted_iota(jnp.int32, (tq, GATE_PAD), 1)
    kmean = kmean_ref[...]

    qh, sel = [], []
    for a in range(HEADS_PER_VREG):
        in_head = (lane >= a * HEAD_DIM) & (lane < (a + 1) * HEAD_DIM)
        qa = jnp.where(in_head, q2, 0.0)
        gate = jnp.dot(qa, kmean, precision=lax.Precision.HIGHEST, preferred_element_type=F32)
        sel.append(_topk_mask(gate, gate_idx < blk).astype(BF16))
        qh.append(qa.astype(BF16))

    def block(ref, j):
        return ref[0, :, pl.ds(pl.multiple_of(j * MOBA_BLOCK, MOBA_BLOCK), MOBA_BLOCK)].astype(BF16)

    r = lax.broadcasted_iota(jnp.int32, (tq, MOBA_BLOCK), 0)
    c = lax.broadcasted_iota(jnp.int32, (tq, MOBA_BLOCK), 1)
    kb = block(kt_ref, blk)
    vb = block(vt_ref, blk)
    init = []
    for a in range(HEADS_PER_VREG):
        s = jnp.where(c <= r, _dot(qh[a], kb) * scale, NEG_INF)
        m = jnp.max(s, axis=1, keepdims=True)
        p = jnp.exp(s - m)
        init += [m, jnp.sum(p, axis=1, keepdims=True), _dot_nt(p.astype(BF16), vb)]

    def body(j, carry):
        kb = block(kt_ref, j)
        vb = block(vt_ref, j)
        onehot = jnp.where(lax.broadcasted_iota(jnp.int32, (GATE_PAD, MOBA_BLOCK), 0) == j, 1.0, 0.0).astype(BF16)
        out = []
        for a in range(HEADS_PER_VREG):
            m, l, acc = carry[3 * a:3 * a + 3]
            picked = _dot(sel[a], onehot) > 0.5
            s = jnp.where(picked, _dot(qh[a], kb) * scale, NEG_INF)
            m_new = jnp.maximum(m, jnp.max(s, axis=1, keepdims=True))
            alpha = jnp.exp(m - m_new)
            p = jnp.exp(s - m_new)
            l = alpha * l + jnp.sum(p, axis=1, keepdims=True)
            acc = alpha * acc + _dot_nt(p.astype(BF16), vb)
            out += [m_new, l, acc]
        return tuple(out)

    res = lax.fori_loop(0, blk, body, tuple(init))
    o = [res[3 * a + 2] / res[3 * a + 1] for a in range(HEADS_PER_VREG)]
    o_ref[0] = jnp.where(lane < HEAD_DIM, o[0], o[1])


def _moba_prompt(q, kt, vt):
    b = q.shape[0]
    kv_spec = pl.BlockSpec((1, LANES, SEQ), lambda bi, hp, qi: (bi, hp, 0))
    q_spec = pl.BlockSpec((1, MOBA_BLOCK, LANES), lambda bi, hp, qi: (bi, qi, hp))
    return pl.pallas_call(
        _moba_prompt_kernel,
        grid=(b, ATT_WIDTH // LANES, SEQ // MOBA_BLOCK),
        in_specs=[q_spec, kv_spec, kv_spec],
        out_specs=q_spec,
        out_shape=jax.ShapeDtypeStruct((b, SEQ, ATT_WIDTH), F32),
        scratch_shapes=[pltpu.VMEM((LANES, GATE_PAD), F32)],
        compiler_params=_params(3, 40),
        name="moba_prompt",
    )(q, kt, vt)


def _moba_sample_kernel(pt_ref, q_ref, kn_ref, vn_ref, ck_hbm, cv_hbm, o_ref,
                        buf, sem, s_ref, p_ref):
    b = pl.program_id(0)
    nb = N_PAST_BLOCKS
    n_rows = N_HEADS * DEC_SEQ
    scale = HEAD_DIM ** -0.5

    def page_copy(cache, g, half):
        slot = lax.rem(g, KV_SLOTS)
        page = pt_ref[b, lax.rem(g, nb) * PAGES_PER_BLOCK + half]
        return pltpu.make_async_copy(
            cache.at[page],
            buf.at[slot, :, pl.ds(half * PAGE_SIZE, PAGE_SIZE)],
            sem.at[slot, half])

    def start_chunk(g):
        @pl.when(g < nb)
        def _():
            for half in range(PAGES_PER_BLOCK):
                page_copy(ck_hbm, g, half).start()

        @pl.when(g >= nb)
        def _():
            for half in range(PAGES_PER_BLOCK):
                page_copy(cv_hbm, g, half).start()

    def wait_chunk(g):
        for half in range(PAGES_PER_BLOCK):
            page_copy(ck_hbm, g, half).wait()

    for g in range(KV_SLOTS):
        start_chunk(jnp.int32(g))

    q = q_ref[0]
    lane_head = lax.broadcasted_iota(jnp.int32, (DEC_SEQ, ATT_WIDTH), 1) // HEAD_DIM
    qbd = jnp.concatenate([jnp.where(lane_head == h, q, 0.0) for h in range(N_HEADS)], axis=0)
    q_hi = qbd.astype(BF16)
    q_lo = (qbd - q_hi.astype(F32)).astype(BF16)
    q_stack = jnp.concatenate([q_hi, q_lo], axis=0)

    def raw_scores(stacked):
        return stacked[0:n_rows] + stacked[n_rows:]

    gate_idx = lax.broadcasted_iota(jnp.int32, (n_rows, GATE_PAD), 1)

    def k_body(g, gate):
        wait_chunk(g)
        kb = buf[lax.rem(g, KV_SLOTS)].astype(BF16)
        s = raw_scores(_dot(q_stack, kb))
        s_ref[g] = s * scale
        start_chunk(g + KV_SLOTS)
        return jnp.where(gate_idx == g, jnp.sum(s, axis=1, keepdims=True) * (1.0 / MOBA_BLOCK), gate)

    gate = lax.fori_loop(0, nb, k_body, jnp.zeros((n_rows, GATE_PAD), F32))
    sel = _topk_mask(gate, gate_idx < nb)

    pad = jnp.zeros((LANES - DEC_SEQ, ATT_WIDTH), F32)
    kn = jnp.concatenate([kn_ref[0], pad], axis=0).astype(BF16)
    vn = jnp.concatenate([vn_ref[0], pad], axis=0).astype(BF16)
    t_row = lax.rem(lax.broadcasted_iota(jnp.int32, (n_rows, LANES), 0), DEC_SEQ)
    t_col = lax.broadcasted_iota(jnp.int32, (n_rows, LANES), 1)
    s_own = jnp.where(t_col <= t_row, raw_scores(_dot_nt(q_stack, kn)) * scale, NEG_INF)

    m = jnp.max(s_own, axis=1, keepdims=True)
    for n in range(nb):
        sn = jnp.where(sel[:, n:n + 1] > 0.5, s_ref[n], NEG_INF)
        m = jnp.maximum(m, jnp.max(sn, axis=1, keepdims=True))
    p_own = jnp.exp(s_own - m)
    l = jnp.sum(p_own, axis=1, keepdims=True)
    for n in range(nb):
        pn = jnp.exp(jnp.where(sel[:, n:n + 1] > 0.5, s_ref[n], NEG_INF) - m)
        l = l + jnp.sum(pn, axis=1, keepdims=True)
        p_ref[n] = pn.astype(BF16)

    def v_body(g, acc):
        wait_chunk(g)
        vb = buf[lax.rem(g, KV_SLOTS)].astype(BF16)
        acc = acc + _dot_nt(p_ref[g - nb], vb)

        @pl.when(g + KV_SLOTS < 2 * nb)
        def _():
            start_chunk(g + KV_SLOTS)

        return acc

    acc = lax.fori_loop(nb, 2 * nb, v_body, _dot(p_own.astype(BF16), vn))
    o = acc / l
    out = jnp.zeros((DEC_SEQ, ATT_WIDTH), F32)
    for h in range(N_HEADS):
        out = jnp.where(lane_head == h, o[h * DEC_SEQ:(h + 1) * DEC_SEQ, :], out)
    o_ref[0] = out


def _moba_sample(q, k_new, v_new, cache_kt, cache_vt, page_table):
    db = q.shape[0]
    n_rows = N_HEADS * DEC_SEQ
    tok_spec = pl.BlockSpec((1, DEC_SEQ, ATT_WIDTH), lambda b, pt: (b, 0, 0))
    grid_spec = pltpu.PrefetchScalarGridSpec(
        num_scalar_prefetch=1,
        grid=(db,),
        in_specs=[tok_spec, tok_spec, tok_spec,
                  pl.BlockSpec(memory_space=pl.ANY), pl.BlockSpec(memory_space=pl.ANY)],
        out_specs=tok_spec,
        scratch_shapes=[
            pltpu.VMEM((KV_SLOTS, ATT_WIDTH, MOBA_BLOCK), F32),
            pltpu.SemaphoreType.DMA((KV_SLOTS, PAGES_PER_BLOCK)),
            pltpu.VMEM((N_PAST_BLOCKS, n_rows, MOBA_BLOCK), F32),
            pltpu.VMEM((N_PAST_BLOCKS, n_rows, MOBA_BLOCK), BF16),
        ],
    )
    return pl.pallas_call(
        _moba_sample_kernel,
        grid_spec=grid_spec,
        out_shape=jax.ShapeDtypeStruct((db, DEC_SEQ, ATT_WIDTH), F32),
        compiler_params=_params(1, 32),
        name="moba_sample",
    )(page_table, q, k_new, v_new, cache_kt, cache_vt)


def _mix_out_kernel(a_ref, b_ref, w_ref, g_ref, r_ref, o_ref):
    y = _dot(a_ref[...].astype(BF16), w_ref[0:POOL_WIDTH, :])
    y = y + _dot(b_ref[...].astype(BF16), w_ref[POOL_WIDTH:, :])
    o_ref[...] = r_ref[...] + _rmsnorm(y, g_ref[...])


def _mix_out(pool, att, w_bf, g, res):
    m = res.shape[0]
    tm = min(ROW_TILE, m)
    half_spec = pl.BlockSpec((tm, POOL_WIDTH), lambda i: (i, 0))
    row_spec = pl.BlockSpec((tm, D_MODEL), lambda i: (i, 0))
    return pl.pallas_call(
        _mix_out_kernel,
        grid=(m // tm,),
        in_specs=[half_spec, half_spec,
                  pl.BlockSpec((D_MODEL, D_MODEL), lambda i: (0, 0)),
                  pl.BlockSpec((1, D_MODEL), lambda i: (0, 0)),
                  row_spec],
        out_specs=row_spec,
        out_shape=jax.ShapeDtypeStruct((m, D_MODEL), F32),
        compiler_params=_params(1, 32),
        name="mix_out",
    )(pool, att, w_bf, g, res)


def _mem_kv_kernel(x_ref, g_ref, wk_ref, wv_ref, k_ref, v_ref):
    n = _rmsnorm(x_ref[...], g_ref[...]).astype(BF16)
    k_ref[...] = _dot(n, wk_ref[...])
    v_ref[...] = _dot(n, wv_ref[...])


def _mem_kv(mem, g, wk_bf, wv_bf):
    m = mem.shape[0]
    tm = MEM_LEN
    row_spec = pl.BlockSpec((tm, D_MODEL), lambda i: (i, 0))
    w_spec = pl.BlockSpec((D_MODEL, D_MODEL), lambda i: (0, 0))
    out = jax.ShapeDtypeStruct((m, D_MODEL), F32)
    return pl.pallas_call(
        _mem_kv_kernel,
        grid=(m // tm,),
        in_specs=[row_spec, pl.BlockSpec((1, D_MODEL), lambda i: (0, 0)), w_spec, w_spec],
        out_specs=[row_spec, row_spec],
        out_shape=[out, out],
        compiler_params=_params(1, 32),
        name="mem_kv",
    )(mem, g, wk_bf, wv_bf)


def _xattn_kernel(h_ref, gpre_ref, wq_ref, mk_ref, mv_ref, wo_ref, gpost_ref, o_ref):
    tm = h_ref.shape[0]
    n_mem = mk_ref.shape[0]
    rows_per_mem = tm // n_mem
    scale = MEM_HEAD_DIM ** -0.5
    h = h_ref[...]
    q = _dot(_rmsnorm(h, gpre_ref[...]).astype(BF16), wq_ref[...])
    y = jnp.zeros((tm, D_MODEL), F32)
    for hh in range(MEM_HEADS):
        cols = slice(hh * MEM_HEAD_DIM, (hh + 1) * MEM_HEAD_DIM)
        qh = q[:, cols].astype(BF16).reshape(n_mem, rows_per_mem, MEM_HEAD_DIM)
        kh = mk_ref[:, :, cols].astype(BF16)
        vh = mv_ref[:, :, cols].astype(BF16)
        s = jnp.einsum("bqd,bkd->bqk", qh, kh, preferred_element_type=F32) * scale
        p = jnp.exp(s - jnp.max(s, axis=-1, keepdims=True))
        p = p / jnp.sum(p, axis=-1, keepdims=True)
        oh = jnp.einsum("bqk,bkd->bqd", p.astype(BF16), vh, preferred_element_type=F32)
        y = y + _dot(oh.reshape(tm, MEM_HEAD_DIM).astype(BF16), wo_ref[cols, :])
    o_ref[...] = h + _rmsnorm(y, gpost_ref[...])


def _xattn(h, gpre, wq_bf, mk, mv, wo_bf, gpost, rows_per_mem, tm):
    m = h.shape[0]
    row_spec = pl.BlockSpec((tm, D_MODEL), lambda i: (i, 0))
    vec_spec = pl.BlockSpec((1, D_MODEL), lambda i: (0, 0))
    w_spec = pl.BlockSpec((D_MODEL, D_MODEL), lambda i: (0, 0))
    if rows_per_mem >= tm:
        tiles_per_mem = rows_per_mem // tm
        mem_spec = pl.BlockSpec((1, MEM_LEN, D_MODEL), lambda i: (i // tiles_per_mem, 0, 0))
    else:
        mem_spec = pl.BlockSpec((tm // rows_per_mem, MEM_LEN, D_MODEL), lambda i: (i, 0, 0))
    return pl.pallas_call(
        _xattn_kernel,
        grid=(m // tm,),
        in_specs=[row_spec, vec_spec, w_spec, mem_spec, mem_spec, w_spec, vec_spec],
        out_specs=row_spec,
        out_shape=jax.ShapeDtypeStruct((m, D_MODEL), F32),
        compiler_params=_params(1, 48),
        name="xattn",
    )(h, gpre, wq_bf, mk, mv, wo_bf, gpost)


def _ffn_kernel(h_ref, gpre_ref, wu_ref, wd_ref, gpost_ref, o_ref):
    h = h_ref[...]
    n = _rmsnorm(h, gpre_ref[...]).astype(BF16)
    y = jnp.zeros(h.shape, F32)
    chunk = D_MODEL
    for c in range(D_FF // chunk):
        a = jnp.square(jnp.maximum(_dot(n, wu_ref[:, c * chunk:(c + 1) * chunk]), 0.0))
        y = y + _dot(a.astype(BF16), wd_ref[c * chunk:(c + 1) * chunk, :])
    o_ref[...] = h + _rmsnorm(y, gpost_ref[...])


def _ffn(h, gpre, wu_bf, wd_bf, gpost):
    m = h.shape[0]
    tm = min(ROW_TILE, m)
    row_spec = pl.BlockSpec((tm, D_MODEL), lambda i: (i, 0))
    vec_spec = pl.BlockSpec((1, D_MODEL), lambda i: (0, 0))
    return pl.pallas_call(
        _ffn_kernel,
        grid=(m // tm,),
        in_specs=[row_spec, vec_spec,
                  pl.BlockSpec((D_MODEL, D_FF), lambda i: (0, 0)),
                  pl.BlockSpec((D_FF, D_MODEL), lambda i: (0, 0)),
                  vec_spec],
        out_specs=row_spec,
        out_shape=jax.ShapeDtypeStruct((m, D_MODEL), F32),
        compiler_params=_params(1, 56),
        name="ffn",
    )(h, gpre, wu_bf, wd_bf, gpost)


def _rope_inv():
    inv = ROPE_THETA ** (-2.0 * jnp.arange(ROT_HALF, dtype=F32) / ROT_DIM)
    d = np.arange(LANES) % HEAD_DIM
    rotary = (d < ROT_DIM).astype(np.float32)
    inv_lanes = (inv[d % ROT_HALF] * rotary).reshape(1, LANES)
    return inv_lanes, inv.reshape(ROT_HALF, 1)


def kernel(x_prompt, x_sample, mem_prompt, cache_k, cache_v, cache_mem_k, cache_mem_v, state_pool, page_table,
           g_pre_mix, w_in, w_pool, pool_scale, w_out, g_post_mix,
           g_mem, g_pre_x, w_xq, w_xk, w_xv, w_xo, g_post_x,
           g_pre_ffn, w_up, w_down, g_post_ffn):
    depth = w_in.shape[0]
    batch = x_prompt.shape[0]
    db = x_sample.shape[0]
    n_phys = cache_k.shape[1]
    inv_lanes, inv_col = _rope_inv()

    hp = x_prompt.reshape(batch * SEQ, D_MODEL)
    hs = x_sample.reshape(db * DEC_SEQ, D_MODEL)
    pool_p, k_p, v_p, mk_p, mv_p, pool_s, k_s, v_s = ([] for _ in range(8))
    for l in range(depth):
        vec = lambda a: a[l].reshape(1, -1)
        w_in_bf = w_in[l].astype(BF16)
        w_uq_bf = w_in_bf[:, :POOL_WIDTH + ATT_WIDTH]
        w_kvt_bf = w_in_bf[:, POOL_WIDTH + ATT_WIDTH:].T
        w_pool_bf = w_pool[l].astype(BF16)
        w_out_bf = w_out[l].astype(BF16)
        w_xq_bf, w_xo_bf = w_xq[l].astype(BF16), w_xo[l].astype(BF16)
        w_up_bf, w_down_bf = w_up[l].astype(BF16), w_down[l].astype(BF16)

        u, q, kt, vt = _in_proj_prompt(hp, vec(g_pre_mix), w_uq_bf, w_kvt_bf, inv_lanes, inv_col, batch)
        pool = _pool_prompt(u, w_pool_bf, vec(pool_scale))
        att = _moba_prompt(q.reshape(batch, SEQ, ATT_WIDTH), kt, vt)
        hp = _mix_out(pool, att.reshape(batch * SEQ, ATT_WIDTH), w_out_bf, vec(g_post_mix), hp)
        mk, mv = _mem_kv(mem_prompt.reshape(batch * MEM_LEN, D_MODEL), vec(g_mem),
                         w_xk[l].astype(BF16), w_xv[l].astype(BF16))
        hp = _xattn(hp, vec(g_pre_x), w_xq_bf, mk.reshape(batch, MEM_LEN, D_MODEL),
                    mv.reshape(batch, MEM_LEN, D_MODEL), w_xo_bf, vec(g_post_x),
                    rows_per_mem=SEQ, tm=ROW_TILE)
        hp = _ffn(hp, vec(g_pre_ffn), w_up_bf, w_down_bf, vec(g_post_ffn))
        pool_p.append(u.reshape(batch, SEQ, POOL_WIDTH)[:, SEQ - POOL_STATE:])
        k_p.append(kt.reshape(batch, N_HEADS, HEAD_DIM, SEQ).transpose(0, 3, 1, 2))
        v_p.append(vt.reshape(batch, N_HEADS, HEAD_DIM, SEQ).transpose(0, 3, 1, 2))
        mk_p.append(mk.reshape(batch, MEM_LEN, MEM_HEADS, MEM_HEAD_DIM))
        mv_p.append(mv.reshape(batch, MEM_LEN, MEM_HEADS, MEM_HEAD_DIM))

        u, q, k, v = _in_proj_rows(hs, vec(g_pre_mix), w_in_bf, inv_lanes, DEC_SEQ, PAST_LEN)
        u_ext = jnp.concatenate([state_pool[l], u.reshape(db, DEC_SEQ, POOL_WIDTH)], axis=1)
        pool = _pool_sample(jnp.pad(u_ext, ((0, 0), (1, 0), (0, 0))), w_pool_bf, vec(pool_scale))
        cache_kt = cache_k[l].transpose(0, 2, 3, 1).reshape(n_phys, ATT_WIDTH, PAGE_SIZE)
        cache_vt = cache_v[l].transpose(0, 2, 3, 1).reshape(n_phys, ATT_WIDTH, PAGE_SIZE)
        att = _moba_sample(q.reshape(db, DEC_SEQ, ATT_WIDTH), k.reshape(db, DEC_SEQ, ATT_WIDTH),
                           v.reshape(db, DEC_SEQ, ATT_WIDTH), cache_kt, cache_vt, page_table)
        hs = _mix_out(pool, att.reshape(db * DEC_SEQ, ATT_WIDTH), w_out_bf, vec(g_post_mix), hs)
        hs = _xattn(hs, vec(g_pre_x), w_xq_bf,
                    cache_mem_k[l].reshape(db, MEM_LEN, D_MODEL), cache_mem_v[l].reshape(db, MEM_LEN, D_MODEL),
                    w_xo_bf, vec(g_post_x), rows_per_mem=DEC_SEQ, tm=4 * DEC_SEQ)
        hs = _ffn(hs, vec(g_pre_ffn), w_up_bf, w_down_bf, vec(g_post_ffn))
        pool_s.append(u_ext[:, u_ext.shape[1] - POOL_STATE:])
        k_s.append(k.reshape(db, DEC_SEQ, N_HEADS, HEAD_DIM))
        v_s.append(v.reshape(db, DEC_SEQ, N_HEADS, HEAD_DIM))

    return (hp.reshape(batch, SEQ, D_MODEL), hs.reshape(db, DEC_SEQ, D_MODEL),
            jnp.stack(pool_p), jnp.stack(k_p), jnp.stack(v_p), jnp.stack(mk_p), jnp.stack(mv_p),
            jnp.stack(pool_s), jnp.stack(k_s), jnp.stack(v_s))
```

```python
import functools

import numpy as np
import jax
import jax.numpy as jnp
from jax import lax
from jax.experimental import pallas as pl
from jax.experimental.pallas import tpu as pltpu

F32 = jnp.float32
BF16 = jnp.bfloat16

D_MODEL = 1024
SEQ = 8192
DEC_SEQ = 8
PAST_LEN = 8192
PAGE_SIZE = 128
POOL_WIDTH = 512
POOL_WINDOWS = (2, 4, 8, 16)
POOL_GW = 128
POOL_STATE = 15
ATT_WIDTH = 512
N_HEADS = 8
HEAD_DIM = 64
ROT_DIM = 16
ROPE_THETA = 500000.0
MOBA_BLOCK = 256
MOBA_TOPK = 3
MEM_LEN = 256
MEM_HEADS = 4
MEM_HEAD_DIM = 256
D_FF = 4096
EPS = 1e-6

LANES = 128
SUBLANES = 8

ROW_TILE = 512
POOL_HALO = 16
N_PAST_BLOCKS = PAST_LEN // MOBA_BLOCK
PAGES_PER_BLOCK = MOBA_BLOCK // PAGE_SIZE
GATE_PAD = LANES
KV_SLOTS = 8
KV_UNROLL = 4
PAST_GROUP = 8
ONES_ROWS = 16
ROT_HALF = ROT_DIM // 2
NEG_INF = float("-inf")
MASK_BIAS = -1e30
LOG2_E = 1.4426950408889634

assert PAST_LEN % MOBA_BLOCK == 0, "sample own-block is assumed to hold new keys only"
assert PAST_LEN + 1 >= max(POOL_WINDOWS)
assert ROT_HALF == SUBLANES, "feature-major rotary assumes one sublane group per rotary half"
assert (2 * N_PAST_BLOCKS) % KV_SLOTS == 0 and N_PAST_BLOCKS % KV_UNROLL == 0 and KV_SLOTS % KV_UNROLL == 0
assert (SEQ // MOBA_BLOCK) % PAST_GROUP == 0 and HEAD_DIM + SEQ // MOBA_BLOCK <= LANES


def _params(n_grid, vmem_mib):
    return pltpu.CompilerParams(
        dimension_semantics=("arbitrary",) * n_grid,
        vmem_limit_bytes=vmem_mib * 1024 * 1024,
    )


def _rmsnorm(x, g):
    ms = jnp.mean(x * x, axis=-1, keepdims=True)
    return x * lax.rsqrt(ms + EPS) * g


def _dot(a, b):
    return jnp.dot(a, b, preferred_element_type=F32)


def _dot_nt(a, b, precision=None):
    return lax.dot_general(a, b, (((1,), (1,)), ((), ())), precision=precision,
                           preferred_element_type=F32)


def _topk_mask(gate, valid, axis):
    n = gate.shape[axis]
    idx = lax.broadcasted_iota(jnp.int32, gate.shape, axis).astype(F32)
    g = jnp.where(valid, gate, NEG_INF)
    sel = jnp.zeros(gate.shape, F32)
    for _ in range(MOBA_TOPK):
        mx = jnp.max(g, axis=axis, keepdims=True)
        first = jnp.min(jnp.where(g == mx, idx, float(n)), axis=axis, keepdims=True)
        pick = idx == first
        sel = jnp.where(pick, 1.0, sel)
        g = jnp.where(pick, NEG_INF, g)
    return jnp.where(valid, sel, 0.0)


def _rope_rows(z, pos_rows, inv_lanes):
    tm = z.shape[0]
    ang = pos_rows * inv_lanes
    cos = jnp.cos(ang)
    sin = jnp.sin(ang)
    d = lax.rem(lax.broadcasted_iota(jnp.int32, (tm, LANES), 1), HEAD_DIM)
    sin_lo = jnp.where(d < ROT_HALF, -sin, 0.0)
    sin_hi = jnp.where((d >= ROT_HALF) & (d < ROT_DIM), sin, 0.0)
    out = []
    for c in range(ATT_WIDTH // LANES):
        zz = z[:, c * LANES:(c + 1) * LANES]
        out.append(zz * cos
                   + pltpu.roll(zz, LANES - ROT_HALF, 1) * sin_lo
                   + pltpu.roll(zz, ROT_HALF, 1) * sin_hi)
    return out


def _in_proj_rows_kernel(x_ref, g_ref, w_ref, inv_ref, u_ref, q_ref, k_ref, v_ref, *, period, offset):
    tm = x_ref.shape[0]
    i = pl.program_id(0)
    n = _rmsnorm(x_ref[...], g_ref[...]).astype(BF16)

    def seg(j):
        return _dot(n, w_ref[:, j * ATT_WIDTH:(j + 1) * ATT_WIDTH])

    u_ref[...] = seg(0)
    v_ref[...] = seg(3)
    row = lax.broadcasted_iota(jnp.int32, (tm, LANES), 0) + i * tm
    pos = (lax.rem(row, period) + offset).astype(F32)
    for j, ref in ((1, q_ref), (2, k_ref)):
        for c, piece in enumerate(_rope_rows(seg(j), pos, inv_ref[...])):
            ref[:, c * LANES:(c + 1) * LANES] = piece


def _in_proj_rows(x, g, w_bf, inv_lanes, period, offset):
    m = x.shape[0]
    tm = min(ROW_TILE, m)
    out = jax.ShapeDtypeStruct((m, ATT_WIDTH), F32)
    row_spec = pl.BlockSpec((tm, ATT_WIDTH), lambda i: (i, 0))
    return pl.pallas_call(
        functools.partial(_in_proj_rows_kernel, period=period, offset=offset),
        grid=(m // tm,),
        in_specs=[
            pl.BlockSpec((tm, D_MODEL), lambda i: (i, 0)),
            pl.BlockSpec((1, D_MODEL), lambda i: (0, 0)),
            pl.BlockSpec((D_MODEL, 4 * ATT_WIDTH), lambda i: (0, 0)),
            pl.BlockSpec((1, LANES), lambda i: (0, 0)),
        ],
        out_specs=[row_spec] * 4,
        out_shape=[out] * 4,
        compiler_params=_params(1, 40),
        name="in_proj_rows",
    )(x, g, w_bf, inv_lanes)


def _in_proj_prompt_kernel(x_ref, g_ref, wu_ref, wqkvt_ref, invc_ref, u_ref, qt_ref, kt_ref, vt_ref):
    tm = x_ref.shape[0]
    i = pl.program_id(1)
    n = _rmsnorm(x_ref[...], g_ref[...]).astype(BF16)
    u_ref[...] = _dot(n, wu_ref[...])
    vt_ref[0] = _dot_nt(wqkvt_ref[2 * ATT_WIDTH:, :], n)
    pos_t = (lax.broadcasted_iota(jnp.int32, (ROT_HALF, tm), 1) + i * tm).astype(F32)
    ang = invc_ref[...] * pos_t
    cos = jnp.cos(ang)
    sin = jnp.sin(ang)
    for off, ref in ((0, qt_ref), (ATT_WIDTH, kt_ref)):
        zt = _dot_nt(wqkvt_ref[off:off + ATT_WIDTH, :], n)
        for h in range(N_HEADS):
            r0 = h * HEAD_DIM
            x1 = zt[r0:r0 + ROT_HALF, :]
            x2 = zt[r0 + ROT_HALF:r0 + ROT_DIM, :]
            ref[0, r0:r0 + ROT_HALF, :] = x1 * cos - x2 * sin
            ref[0, r0 + ROT_HALF:r0 + ROT_DIM, :] = x2 * cos + x1 * sin
            ref[0, r0 + ROT_DIM:r0 + HEAD_DIM, :] = zt[r0 + ROT_DIM:r0 + HEAD_DIM, :]


def _in_proj_prompt(x, g, wu_bf, wqkvt_bf, inv_col, batch):
    tm = ROW_TILE
    tiles = SEQ // tm
    t_spec = pl.BlockSpec((1, ATT_WIDTH, tm), lambda b, i: (b, 0, i))
    feat = jax.ShapeDtypeStruct((batch, ATT_WIDTH, SEQ), F32)
    return pl.pallas_call(
        _in_proj_prompt_kernel,
        grid=(batch, tiles),
        in_specs=[
            pl.BlockSpec((tm, D_MODEL), lambda b, i: (b * tiles + i, 0)),
            pl.BlockSpec((1, D_MODEL), lambda b, i: (0, 0)),
            pl.BlockSpec((D_MODEL, POOL_WIDTH), lambda b, i: (0, 0)),
            pl.BlockSpec((3 * ATT_WIDTH, D_MODEL), lambda b, i: (0, 0)),
            pl.BlockSpec((ROT_HALF, 1), lambda b, i: (0, 0)),
        ],
        out_specs=[pl.BlockSpec((tm, POOL_WIDTH), lambda b, i: (b * tiles + i, 0)), t_spec, t_spec, t_spec],
        out_shape=[jax.ShapeDtypeStruct((batch * SEQ, POOL_WIDTH), F32), feat, feat, feat],
        compiler_params=_params(2, 40),
        name="in_proj_prompt",
    )(x, g, wu_bf, wqkvt_bf, inv_col)


def _pool_prompt_kernel(u_ref, prev_ref, w_ref, sc_ref, o_ref, ext_ref):
    tm = u_ref.shape[0]
    i = pl.program_id(0)
    pos0 = lax.rem(i * tm, SEQ)
    ext_ref[0:POOL_HALO, :] = jnp.where(pos0 == 0, 0.0, prev_ref[...])
    ext_ref[POOL_HALO:, :] = u_ref[...]
    pos = pos0 + lax.broadcasted_iota(jnp.int32, (tm, POOL_GW), 0)
    for g, w in enumerate(POOL_WINDOWS):
        cols = slice(g * POOL_GW, (g + 1) * POOL_GW)
        cur = u_ref[:, cols]
        s = cur
        for k in range(1, w):
            s = s + ext_ref[POOL_HALO - k:POOL_HALO - k + tm, cols]
        cnt = jnp.minimum(pos + 1, w).astype(F32)
        d = s / cnt - cur
        o_ref[:, cols] = _dot(d.astype(BF16), w_ref[g]) * sc_ref[:, cols]


def _pool_prompt(u, w_pool_bf, pool_scale):
    m = u.shape[0]
    tm = ROW_TILE
    halo_per_tile = tm // POOL_HALO
    return pl.pallas_call(
        _pool_prompt_kernel,
        grid=(m // tm,),
        in_specs=[
            pl.BlockSpec((tm, POOL_WIDTH), lambda i: (i, 0)),
            pl.BlockSpec((POOL_HALO, POOL_WIDTH), lambda i: (jnp.maximum(i * halo_per_tile - 1, 0), 0)),
            pl.BlockSpec((len(POOL_WINDOWS), POOL_GW, POOL_GW), lambda i: (0, 0, 0)),
            pl.BlockSpec((1, POOL_WIDTH), lambda i: (0, 0)),
        ],
        out_specs=pl.BlockSpec((tm, POOL_WIDTH), lambda i: (i, 0)),
        out_shape=jax.ShapeDtypeStruct((m, POOL_WIDTH), F32),
        scratch_shapes=[pltpu.VMEM((POOL_HALO + tm, POOL_WIDTH), F32)],
        compiler_params=_params(1, 32),
        name="pool_prompt",
    )(u, u, w_pool_bf, pool_scale)


def _pool_sample_kernel(e_ref, w_ref, sc_ref, o_ref):
    nb = e_ref.shape[0]
    lo = 1 + POOL_STATE
    t = lax.broadcasted_iota(jnp.int32, (nb, DEC_SEQ, POOL_GW), 1)
    for g, w in enumerate(POOL_WINDOWS):
        cols = slice(g * POOL_GW, (g + 1) * POOL_GW)
        cur = e_ref[:, lo:lo + DEC_SEQ, cols]
        s = cur
        for k in range(1, w):
            s = s + e_ref[:, lo - k:lo - k + DEC_SEQ, cols]
        cnt = jnp.minimum(PAST_LEN + t + 1, w).astype(F32)
        d = (s / cnt - cur).reshape(nb * DEC_SEQ, POOL_GW)
        o_ref[:, cols] = _dot(d.astype(BF16), w_ref[g]) * sc_ref[:, cols]


def _pool_sample(ext, w_pool_bf, pool_scale):
    db, rows, _ = ext.shape
    nb = 32
    return pl.pallas_call(
        _pool_sample_kernel,
        grid=(db // nb,),
        in_specs=[
            pl.BlockSpec((nb, rows, POOL_WIDTH), lambda i: (i, 0, 0)),
            pl.BlockSpec((len(POOL_WINDOWS), POOL_GW, POOL_GW), lambda i: (0, 0, 0)),
            pl.BlockSpec((1, POOL_WIDTH), lambda i: (0, 0)),
        ],
        out_specs=pl.BlockSpec((nb * DEC_SEQ, POOL_WIDTH), lambda i: (i, 0)),
        out_shape=jax.ShapeDtypeStruct((db * DEC_SEQ, POOL_WIDTH), F32),
        compiler_params=_params(1, 32),
        name="pool_sample",
    )(ext, w_pool_bf, pool_scale)


def _moba_prompt_kernel(qt_ref, kt_ref, vt_ref, o_ref, kaug_ref, kmean_ref):
    tq = MOBA_BLOCK
    blk = pl.program_id(2)
    n_blocks = SEQ // MOBA_BLOCK
    pad_rows = LANES - HEAD_DIM - n_blocks

    @pl.when(blk == 0)
    def _():
        block_row = lax.broadcasted_iota(jnp.int32, (n_blocks, MOBA_BLOCK), 0)
        lane = lax.broadcasted_iota(jnp.int32, (1, LANES), 1)
        for n in range(n_blocks):
            kb = kt_ref[0, :, n * MOBA_BLOCK:(n + 1) * MOBA_BLOCK]
            aug = jnp.concatenate([kb, jnp.where(block_row == n, 1.0, 0.0),
                                   jnp.zeros((pad_rows, MOBA_BLOCK), F32)], axis=0).T
            kaug_ref[n * MOBA_BLOCK:(n + 1) * MOBA_BLOCK, :] = aug.astype(BF16)
            kmean_ref[n:n + 1, :] = jnp.where(lane < HEAD_DIM,
                                               jnp.sum(aug, axis=0, keepdims=True) * (1.0 / MOBA_BLOCK), 0.0)

    qt = qt_ref[0]
    gate = jnp.dot(kmean_ref[...], jnp.concatenate([qt, jnp.zeros((LANES - HEAD_DIM, tq), F32)], axis=0),
                   precision=lax.Precision.HIGHEST, preferred_element_type=F32)
    n_idx = lax.broadcasted_iota(jnp.int32, (n_blocks, tq), 0)
    sel = _topk_mask(gate, n_idx < blk, axis=0)
    bias = jnp.where(sel > 0.5, 0.0, MASK_BIAS)
    q_log2 = qt * (HEAD_DIM ** -0.5 * LOG2_E)
    w_past = jnp.concatenate([q_log2, bias, jnp.zeros((pad_rows, tq), F32)], axis=0).astype(BF16)
    w_own = jnp.concatenate([q_log2, jnp.zeros((LANES - HEAD_DIM, tq), F32)], axis=0).astype(BF16)

    def scores(w, start, size):
        return _dot(kaug_ref[pl.ds(pl.multiple_of(start, MOBA_BLOCK), size), :], w)

    def values(start, size):
        vb = vt_ref[0, :, pl.ds(pl.multiple_of(start, MOBA_BLOCK), size)]
        return jnp.concatenate([vb.astype(BF16), jnp.ones((ONES_ROWS, size), BF16)], axis=0)

    key_pos = lax.broadcasted_iota(jnp.int32, (MOBA_BLOCK, tq), 0)
    q_pos = lax.broadcasted_iota(jnp.int32, (MOBA_BLOCK, tq), 1)
    s = jnp.where(key_pos <= q_pos, scores(w_own, blk * MOBA_BLOCK, MOBA_BLOCK), NEG_INF)
    m0 = jnp.max(s, axis=0, keepdims=True)
    acc0 = _dot(values(blk * MOBA_BLOCK, MOBA_BLOCK), jnp.exp2(s - m0).astype(BF16))

    span = PAST_GROUP * MOBA_BLOCK

    def body(i, carry):
        m, acc = carry
        s = scores(w_past, i * span, span)
        m_new = jnp.maximum(m, jnp.max(s, axis=0, keepdims=True))
        p = jnp.exp2(s - m_new).astype(BF16)
        return m_new, jnp.exp2(m - m_new) * acc + _dot(values(i * span, span), p)

    _, acc = lax.fori_loop(0, (blk + PAST_GROUP - 1) // PAST_GROUP, body, (m0, acc0))
    o_ref[0] = acc[0:HEAD_DIM, :] / acc[HEAD_DIM:HEAD_DIM + 1, :]


def _moba_prompt(qt, kt, vt):
    b = qt.shape[0]
    kv_spec = pl.BlockSpec((1, HEAD_DIM, SEQ), lambda bi, h, qi: (bi, h, 0))
    q_spec = pl.BlockSpec((1, HEAD_DIM, MOBA_BLOCK), lambda bi, h, qi: (bi, h, qi))
    return pl.pallas_call(
        _moba_prompt_kernel,
        grid=(b, N_HEADS, SEQ // MOBA_BLOCK),
        in_specs=[q_spec, kv_spec, kv_spec],
        out_specs=q_spec,
        out_shape=jax.ShapeDtypeStruct((b, ATT_WIDTH, SEQ), F32),
        scratch_shapes=[pltpu.VMEM((SEQ, LANES), BF16),
                        pltpu.VMEM((SEQ // MOBA_BLOCK, LANES), F32)],
        compiler_params=_params(3, 40),
        name="moba_prompt",
    )(qt, kt, vt)


def _moba_sample_kernel(pt_ref, q_ref, kn_ref, vn_ref, ck_hbm, cv_hbm, o_ref,
                        buf, sem, s_ref, p_ref):
    b = pl.program_id(0)
    n_seq = pl.num_programs(0)
    nb = N_PAST_BLOCKS
    n_rows = N_HEADS * DEC_SEQ
    scale = HEAD_DIM ** -0.5

    def page_copy(cache, seq, g, half):
        slot = lax.rem(g, KV_SLOTS)
        page = pt_ref[seq, lax.rem(g, nb) * PAGES_PER_BLOCK + half]
        return pltpu.make_async_copy(
            cache.at[page],
            buf.at[slot, :, pl.ds(half * PAGE_SIZE, PAGE_SIZE)],
            sem.at[slot, half])

    def start_chunk(seq, g):
        @pl.when(g < nb)
        def _():
            for half in range(PAGES_PER_BLOCK):
                page_copy(ck_hbm, seq, g, half).start()

        @pl.when(g >= nb)
        def _():
            for half in range(PAGES_PER_BLOCK):
                page_copy(cv_hbm, seq, g, half).start()

    def start_ahead(g):
        nxt = g + KV_SLOTS

        @pl.when(nxt < 2 * nb)
        def _():
            start_chunk(b, nxt)

        @pl.when((nxt >= 2 * nb) & (b + 1 < n_seq))
        def _():
            start_chunk(b + 1, nxt - 2 * nb)

    def wait_chunk(g):
        for half in range(PAGES_PER_BLOCK):
            page_copy(ck_hbm, b, g, half).wait()

    @pl.when(b == 0)
    def _():
        for g in range(KV_SLOTS):
            start_chunk(b, jnp.int32(g))

    q = q_ref[0]
    lane_head = lax.broadcasted_iota(jnp.int32, (DEC_SEQ, ATT_WIDTH), 1) // HEAD_DIM
    qbd = jnp.concatenate([jnp.where(lane_head == h, q, 0.0) for h in range(N_HEADS)], axis=0)
    q_hi = qbd.astype(BF16)
    q_lo = (qbd - q_hi.astype(F32)).astype(BF16)
    q_stack = jnp.concatenate([q_hi, q_lo], axis=0)

    def raw_scores(stacked):
        return stacked[0:n_rows] + stacked[n_rows:]

    gate_idx = lax.broadcasted_iota(jnp.int32, (n_rows, GATE_PAD), 1)

    def k_body(i, gate):
        chunks = [i * KV_UNROLL + u for u in range(KV_UNROLL)]
        for g in chunks:
            wait_chunk(g)
        for g in chunks:
            kb = buf[lax.rem(g, KV_SLOTS)].astype(BF16)
            s = raw_scores(_dot(q_stack, kb))
            s_ref[g] = s * scale
            gate = jnp.where(gate_idx == g, jnp.sum(s, axis=1, keepdims=True) * (1.0 / MOBA_BLOCK), gate)
        for g in chunks:
            start_ahead(g)
        return gate

    gate = lax.fori_loop(0, nb // KV_UNROLL, k_body, jnp.zeros((n_rows, GATE_PAD), F32))
    sel = _topk_mask(gate, gate_idx < nb, axis=1)

    pad = jnp.zeros((LANES - DEC_SEQ, ATT_WIDTH), F32)
    kn = jnp.concatenate([kn_ref[0], pad], axis=0).astype(BF16)
    vn = jnp.concatenate([vn_ref[0], pad], axis=0).astype(BF16)
    t_row = lax.rem(lax.broadcasted_iota(jnp.int32, (n_rows, LANES), 0), DEC_SEQ)
    t_col = lax.broadcasted_iota(jnp.int32, (n_rows, LANES), 1)
    s_own = jnp.where(t_col <= t_row, raw_scores(_dot_nt(q_stack, kn)) * scale, NEG_INF)

    m_lanes = jnp.full((n_rows, MOBA_BLOCK), NEG_INF, F32)
    for n in range(nb):
        m_lanes = jnp.maximum(m_lanes, jnp.where(sel[:, n:n + 1] > 0.5, s_ref[n], NEG_INF))
    m = jnp.maximum(jnp.max(s_own, axis=1, keepdims=True), jnp.max(m_lanes, axis=1, keepdims=True))
    p_own = jnp.exp(s_own - m)
    l_lanes = jnp.zeros((n_rows, MOBA_BLOCK), F32)
    for n in range(nb):
        pn = jnp.exp(jnp.where(sel[:, n:n + 1] > 0.5, s_ref[n], NEG_INF) - m)
        l_lanes = l_lanes + pn
        p_ref[n] = pn.astype(BF16)
    l = jnp.sum(p_own, axis=1, keepdims=True) + jnp.sum(l_lanes, axis=1, keepdims=True)

    def v_body(i, acc):
        chunks = [nb + i * KV_UNROLL + u for u in range(KV_UNROLL)]
        for g in chunks:
            wait_chunk(g)
        for g in chunks:
            vb = buf[lax.rem(g, KV_SLOTS)].astype(BF16)
            acc = acc + _dot_nt(p_ref[g - nb], vb)
        for g in chunks:
            start_ahead(g)
        return acc

    acc = lax.fori_loop(0, nb // KV_UNROLL, v_body, _dot(p_own.astype(BF16), vn))
    o = acc / l
    out = jnp.zeros((DEC_SEQ, ATT_WIDTH), F32)
    for h in range(N_HEADS):
        out = jnp.where(lane_head == h, o[h * DEC_SEQ:(h + 1) * DEC_SEQ, :], out)
    o_ref[0] = out


def _moba_sample(q, k_new, v_new, cache_kt, cache_vt, page_table):
    db = q.shape[0]
    n_rows = N_HEADS * DEC_SEQ
    tok_spec = pl.BlockSpec((1, DEC_SEQ, ATT_WIDTH), lambda b, pt: (b, 0, 0))
    grid_spec = pltpu.PrefetchScalarGridSpec(
        num_scalar_prefetch=1,
        grid=(db,),
        in_specs=[tok_spec, tok_spec, tok_spec,
                  pl.BlockSpec(memory_space=pl.ANY), pl.BlockSpec(memory_space=pl.ANY)],
        out_specs=tok_spec,
        scratch_shapes=[
            pltpu.VMEM((KV_SLOTS, ATT_WIDTH, MOBA_BLOCK), F32),
            pltpu.SemaphoreType.DMA((KV_SLOTS, PAGES_PER_BLOCK)),
            pltpu.VMEM((N_PAST_BLOCKS, n_rows, MOBA_BLOCK), F32),
            pltpu.VMEM((N_PAST_BLOCKS, n_rows, MOBA_BLOCK), BF16),
        ],
    )
    return pl.pallas_call(
        _moba_sample_kernel,
        grid_spec=grid_spec,
        out_shape=jax.ShapeDtypeStruct((db, DEC_SEQ, ATT_WIDTH), F32),
        compiler_params=_params(1, 32),
        name="moba_sample",
    )(page_table, q, k_new, v_new, cache_kt, cache_vt)


def _mix_out_kernel(a_ref, b_ref, w_ref, g_ref, r_ref, o_ref):
    att = b_ref[0].T if len(b_ref.shape) == 3 else b_ref[...]
    y = _dot(a_ref[...].astype(BF16), w_ref[0:POOL_WIDTH, :])
    y = y + _dot(att.astype(BF16), w_ref[POOL_WIDTH:, :])
    o_ref[...] = r_ref[...] + _rmsnorm(y, g_ref[...])


def _mix_out(pool, att, w_bf, g, res):
    m = res.shape[0]
    tm = min(ROW_TILE, m)
    half_spec = pl.BlockSpec((tm, POOL_WIDTH), lambda i: (i, 0))
    row_spec = pl.BlockSpec((tm, D_MODEL), lambda i: (i, 0))
    if att.ndim == 3:
        tiles = att.shape[2] // tm
        att_spec = pl.BlockSpec((1, ATT_WIDTH, tm), lambda i: (i // tiles, 0, i % tiles))
    else:
        att_spec = half_spec
    return pl.pallas_call(
        _mix_out_kernel,
        grid=(m // tm,),
        in_specs=[half_spec, att_spec,
                  pl.BlockSpec((D_MODEL, D_MODEL), lambda i: (0, 0)),
                  pl.BlockSpec((1, D_MODEL), lambda i: (0, 0)),
                  row_spec],
        out_specs=row_spec,
        out_shape=jax.ShapeDtypeStruct((m, D_MODEL), F32),
        compiler_params=_params(1, 32),
        name="mix_out",
    )(pool, att, w_bf, g, res)


def _mem_kv_kernel(x_ref, g_ref, wk_ref, wv_ref, k_ref, v_ref):
    n = _rmsnorm(x_ref[...], g_ref[...]).astype(BF16)
    k_ref[...] = _dot(n, wk_ref[...])
    v_ref[...] = _dot(n, wv_ref[...])


def _mem_kv(mem, g, wk_bf, wv_bf):
    m = mem.shape[0]
    tm = MEM_LEN
    row_spec = pl.BlockSpec((tm, D_MODEL), lambda i: (i, 0))
    w_spec = pl.BlockSpec((D_MODEL, D_MODEL), lambda i: (0, 0))
    out = jax.ShapeDtypeStruct((m, D_MODEL), F32)
    return pl.pallas_call(
        _mem_kv_kernel,
        grid=(m // tm,),
        in_specs=[row_spec, pl.BlockSpec((1, D_MODEL), lambda i: (0, 0)), w_spec, w_spec],
        out_specs=[row_spec, row_spec],
        out_shape=[out, out],
        compiler_params=_params(1, 32),
        name="mem_kv",
    )(mem, g, wk_bf, wv_bf)


def _xattn_kernel(h_ref, gpre_ref, wq_ref, mk_ref, mv_ref, wo_ref, gpost_ref, o_ref):
    tm = h_ref.shape[0]
    n_mem = mk_ref.shape[0]
    rows_per_mem = tm // n_mem
    scale = MEM_HEAD_DIM ** -0.5
    h = h_ref[...]
    q = _dot(_rmsnorm(h, gpre_ref[...]).astype(BF16), wq_ref[...])
    y = jnp.zeros((tm, D_MODEL), F32)
    for hh in range(MEM_HEADS):
        cols = slice(hh * MEM_HEAD_DIM, (hh + 1) * MEM_HEAD_DIM)
        qh = q[:, cols].astype(BF16).reshape(n_mem, rows_per_mem, MEM_HEAD_DIM)
        kh = mk_ref[:, :, cols].astype(BF16)
        vh = mv_ref[:, :, cols].astype(BF16)
        s = jnp.einsum("bqd,bkd->bqk", qh, kh, preferred_element_type=F32) * scale
        p = jnp.exp(s - jnp.max(s, axis=-1, keepdims=True))
        p = p / jnp.sum(p, axis=-1, keepdims=True)
        oh = jnp.einsum("bqk,bkd->bqd", p.astype(BF16), vh, preferred_element_type=F32)
        y = y + _dot(oh.reshape(tm, MEM_HEAD_DIM).astype(BF16), wo_ref[cols, :])
    o_ref[...] = h + _rmsnorm(y, gpost_ref[...])


def _xattn(h, gpre, wq_bf, mk, mv, wo_bf, gpost, rows_per_mem, tm):
    m = h.shape[0]
    row_spec = pl.BlockSpec((tm, D_MODEL), lambda i: (i, 0))
    vec_spec = pl.BlockSpec((1, D_MODEL), lambda i: (0, 0))
    w_spec = pl.BlockSpec((D_MODEL, D_MODEL), lambda i: (0, 0))
    if rows_per_mem >= tm:
        tiles_per_mem = rows_per_mem // tm
        mem_spec = pl.BlockSpec((1, MEM_LEN, D_MODEL), lambda i: (i // tiles_per_mem, 0, 0))
    else:
        mem_spec = pl.BlockSpec((tm // rows_per_mem, MEM_LEN, D_MODEL), lambda i: (i, 0, 0))
    return pl.pallas_call(
        _xattn_kernel,
        grid=(m // tm,),
        in_specs=[row_spec, vec_spec, w_spec, mem_spec, mem_spec, w_spec, vec_spec],
        out_specs=row_spec,
        out_shape=jax.ShapeDtypeStruct((m, D_MODEL), F32),
        compiler_params=_params(1, 48),
        name="xattn",
    )(h, gpre, wq_bf, mk, mv, wo_bf, gpost)


def _ffn_kernel(h_ref, gpre_ref, wu_ref, wd_ref, gpost_ref, o_ref):
    h = h_ref[...]
    n = _rmsnorm(h, gpre_ref[...]).astype(BF16)
    y = jnp.zeros(h.shape, F32)
    chunk = D_MODEL
    for c in range(D_FF // chunk):
        a = jnp.square(jnp.maximum(_dot(n, wu_ref[:, c * chunk:(c + 1) * chunk]), 0.0))
        y = y + _dot(a.astype(BF16), wd_ref[c * chunk:(c + 1) * chunk, :])
    o_ref[...] = h + _rmsnorm(y, gpost_ref[...])


def _ffn(h, gpre, wu_bf, wd_bf, gpost):
    m = h.shape[0]
    tm = min(ROW_TILE, m)
    row_spec = pl.BlockSpec((tm, D_MODEL), lambda i: (i, 0))
    vec_spec = pl.BlockSpec((1, D_MODEL), lambda i: (0, 0))
    return pl.pallas_call(
        _ffn_kernel,
        grid=(m // tm,),
        in_specs=[row_spec, vec_spec,
                  pl.BlockSpec((D_MODEL, D_FF), lambda i: (0, 0)),
                  pl.BlockSpec((D_FF, D_MODEL), lambda i: (0, 0)),
                  vec_spec],
        out_specs=row_spec,
        out_shape=jax.ShapeDtypeStruct((m, D_MODEL), F32),
        compiler_params=_params(1, 56),
        name="ffn",
    )(h, gpre, wu_bf, wd_bf, gpost)


def _rope_inv():
    inv = ROPE_THETA ** (-2.0 * jnp.arange(ROT_HALF, dtype=F32) / ROT_DIM)
    d = np.arange(LANES) % HEAD_DIM
    rotary = (d < ROT_DIM).astype(np.float32)
    inv_lanes = (inv[d % ROT_HALF] * rotary).reshape(1, LANES)
    return inv_lanes, inv.reshape(ROT_HALF, 1)


def kernel(x_prompt, x_sample, mem_prompt, cache_k, cache_v, cache_mem_k, cache_mem_v, state_pool, page_table,
           g_pre_mix, w_in, w_pool, pool_scale, w_out, g_post_mix,
           g_mem, g_pre_x, w_xq, w_xk, w_xv, w_xo, g_post_x,
           g_pre_ffn, w_up, w_down, g_post_ffn):
    depth = w_in.shape[0]
    batch = x_prompt.shape[0]
    db = x_sample.shape[0]
    n_phys = cache_k.shape[1]
    inv_lanes, inv_col = _rope_inv()

    hp = x_prompt.reshape(batch * SEQ, D_MODEL)
    hs = x_sample.reshape(db * DEC_SEQ, D_MODEL)
    pool_p, k_p, v_p, mk_p, mv_p, pool_s, k_s, v_s = ([] for _ in range(8))
    for l in range(depth):
        vec = lambda a: a[l].reshape(1, -1)
        w_in_bf = w_in[l].astype(BF16)
        w_u_bf = w_in_bf[:, :POOL_WIDTH]
        w_qkvt_bf = w_in_bf[:, POOL_WIDTH:].T
        w_pool_bf = w_pool[l].astype(BF16)
        w_out_bf = w_out[l].astype(BF16)
        w_xq_bf, w_xo_bf = w_xq[l].astype(BF16), w_xo[l].astype(BF16)
        w_up_bf, w_down_bf = w_up[l].astype(BF16), w_down[l].astype(BF16)

        u, qt, kt, vt = _in_proj_prompt(hp, vec(g_pre_mix), w_u_bf, w_qkvt_bf, inv_col, batch)
        pool = _pool_prompt(u, w_pool_bf, vec(pool_scale))
        hp = _mix_out(pool, _moba_prompt(qt, kt, vt), w_out_bf, vec(g_post_mix), hp)
        mk, mv = _mem_kv(mem_prompt.reshape(batch * MEM_LEN, D_MODEL), vec(g_mem),
                         w_xk[l].astype(BF16), w_xv[l].astype(BF16))
        hp = _xattn(hp, vec(g_pre_x), w_xq_bf, mk.reshape(batch, MEM_LEN, D_MODEL),
                    mv.reshape(batch, MEM_LEN, D_MODEL), w_xo_bf, vec(g_post_x),
                    rows_per_mem=SEQ, tm=ROW_TILE)
        hp = _ffn(hp, vec(g_pre_ffn), w_up_bf, w_down_bf, vec(g_post_ffn))
        pool_p.append(u.reshape(batch, SEQ, POOL_WIDTH)[:, SEQ - POOL_STATE:])
        k_p.append(kt.reshape(batch, N_HEADS, HEAD_DIM, SEQ).transpose(0, 3, 1, 2))
        v_p.append(vt.reshape(batch, N_HEADS, HEAD_DIM, SEQ).transpose(0, 3, 1, 2))
        mk_p.append(mk.reshape(batch, MEM_LEN, MEM_HEADS, MEM_HEAD_DIM))
        mv_p.append(mv.reshape(batch, MEM_LEN, MEM_HEADS, MEM_HEAD_DIM))

        u, q, k, v = _in_proj_rows(hs, vec(g_pre_mix), w_in_bf, inv_lanes, DEC_SEQ, PAST_LEN)
        u_ext = jnp.concatenate([state_pool[l], u.reshape(db, DEC_SEQ, POOL_WIDTH)], axis=1)
        pool = _pool_sample(jnp.pad(u_ext, ((0, 0), (1, 0), (0, 0))), w_pool_bf, vec(pool_scale))
        cache_kt = cache_k[l].transpose(0, 2, 3, 1).reshape(n_phys, ATT_WIDTH, PAGE_SIZE)
        cache_vt = cache_v[l].transpose(0, 2, 3, 1).reshape(n_phys, ATT_WIDTH, PAGE_SIZE)
        att = _moba_sample(q.reshape(db, DEC_SEQ, ATT_WIDTH), k.reshape(db, DEC_SEQ, ATT_WIDTH),
                           v.reshape(db, DEC_SEQ, ATT_WIDTH), cache_kt, cache_vt, page_table)
        hs = _mix_out(pool, att.reshape(db * DEC_SEQ, ATT_WIDTH), w_out_bf, vec(g_post_mix), hs)
        hs = _xattn(hs, vec(g_pre_x), w_xq_bf,
                    cache_mem_k[l].reshape(db, MEM_LEN, D_MODEL), cache_mem_v[l].reshape(db, MEM_LEN, D_MODEL),
                    w_xo_bf, vec(g_post_x), rows_per_mem=DEC_SEQ, tm=4 * DEC_SEQ)
        hs = _ffn(hs, vec(g_pre_ffn), w_up_bf, w_down_bf, vec(g_post_ffn))
        pool_s.append(u_ext[:, u_ext.shape[1] - POOL_STATE:])
        k_s.append(k.reshape(db, DEC_SEQ, N_HEADS, HEAD_DIM))
        v_s.append(v.reshape(db, DEC_SEQ, N_HEADS, HEAD_DIM))

    return (hp.reshape(batch, SEQ, D_MODEL), hs.reshape(db, DEC_SEQ, D_MODEL),
            jnp.stack(pool_p), jnp.stack(k_p), jnp.stack(v_p), jnp.stack(mk_p), jnp.stack(mv_p),
            jnp.stack(pool_s), jnp.stack(k_s), jnp.stack(v_s))
```

```python
import functools

import numpy as np
import jax
import jax.numpy as jnp
from jax import lax
from jax.experimental import pallas as pl
from jax.experimental.pallas import tpu as pltpu

F32 = jnp.float32
BF16 = jnp.bfloat16

D_MODEL = 1024
SEQ = 8192
DEC_SEQ = 8
PAST_LEN = 8192
PAGE_SIZE = 128
POOL_WIDTH = 512
POOL_WINDOWS = (2, 4, 8, 16)
POOL_GW = 128
POOL_STATE = 15
ATT_WIDTH = 512
N_HEADS = 8
HEAD_DIM = 64
ROT_DIM = 16
ROPE_THETA = 500000.0
MOBA_BLOCK = 256
MOBA_TOPK = 3
MEM_LEN = 256
MEM_HEADS = 4
MEM_HEAD_DIM = 256
D_FF = 4096
EPS = 1e-6

LANES = 128
SUBLANES = 8

ROW_TILE = 512
POOL_HALO = 16
N_PAST_BLOCKS = PAST_LEN // MOBA_BLOCK
PAGES_PER_BLOCK = MOBA_BLOCK // PAGE_SIZE
GATE_PAD = LANES
KV_SLOTS = 32
KV_UNROLL = 4
PAST_GROUP = 8
PAST_STEP = 2
ONES_ROWS = 16
ROT_HALF = ROT_DIM // 2
NEG_INF = float("-inf")
MASK_BIAS = -1e30
LOG2_E = 1.4426950408889634

assert PAST_LEN % MOBA_BLOCK == 0, "sample own-block is assumed to hold new keys only"
assert PAST_LEN + 1 >= max(POOL_WINDOWS)
assert ROT_HALF == SUBLANES, "feature-major rotary assumes one sublane group per rotary half"
assert (2 * N_PAST_BLOCKS) % KV_SLOTS == 0 and N_PAST_BLOCKS % KV_UNROLL == 0 and KV_SLOTS % KV_UNROLL == 0
assert (SEQ // MOBA_BLOCK) % PAST_GROUP == 0 and HEAD_DIM + SEQ // MOBA_BLOCK <= LANES


def _params(n_grid, vmem_mib):
    return pltpu.CompilerParams(
        dimension_semantics=("arbitrary",) * n_grid,
        vmem_limit_bytes=vmem_mib * 1024 * 1024,
    )


def _rmsnorm(x, g):
    ms = jnp.mean(x * x, axis=-1, keepdims=True)
    return x * lax.rsqrt(ms + EPS) * g


def _dot(a, b):
    return jnp.dot(a, b, preferred_element_type=F32)


def _dot_nt(a, b, precision=None):
    return lax.dot_general(a, b, (((1,), (1,)), ((), ())), precision=precision,
                           preferred_element_type=F32)


def _topk_mask(gate, valid, axis):
    n = gate.shape[axis]
    idx = lax.broadcasted_iota(jnp.int32, gate.shape, axis).astype(F32)
    g = jnp.where(valid, gate, NEG_INF)
    sel = jnp.zeros(gate.shape, F32)
    for _ in range(MOBA_TOPK):
        mx = jnp.max(g, axis=axis, keepdims=True)
        first = jnp.min(jnp.where(g == mx, idx, float(n)), axis=axis, keepdims=True)
        pick = idx == first
        sel = jnp.where(pick, 1.0, sel)
        g = jnp.where(pick, NEG_INF, g)
    return jnp.where(valid, sel, 0.0)


def _rope_rows(z, pos_rows, inv_lanes):
    tm = z.shape[0]
    ang = pos_rows * inv_lanes
    cos = jnp.cos(ang)
    sin = jnp.sin(ang)
    d = lax.rem(lax.broadcasted_iota(jnp.int32, (tm, LANES), 1), HEAD_DIM)
    sin_lo = jnp.where(d < ROT_HALF, -sin, 0.0)
    sin_hi = jnp.where((d >= ROT_HALF) & (d < ROT_DIM), sin, 0.0)
    out = []
    for c in range(ATT_WIDTH // LANES):
        zz = z[:, c * LANES:(c + 1) * LANES]
        out.append(zz * cos
                   + pltpu.roll(zz, LANES - ROT_HALF, 1) * sin_lo
                   + pltpu.roll(zz, ROT_HALF, 1) * sin_hi)
    return out


def _in_proj_rows_kernel(x_ref, g_ref, w_ref, inv_ref, u_ref, q_ref, k_ref, v_ref, *, period, offset):
    tm = x_ref.shape[0]
    i = pl.program_id(0)
    n = _rmsnorm(x_ref[...], g_ref[...]).astype(BF16)

    def seg(j):
        return _dot(n, w_ref[:, j * ATT_WIDTH:(j + 1) * ATT_WIDTH])

    u_ref[...] = seg(0)
    v_ref[...] = seg(3)
    row = lax.broadcasted_iota(jnp.int32, (tm, LANES), 0) + i * tm
    pos = (lax.rem(row, period) + offset).astype(F32)
    for j, ref in ((1, q_ref), (2, k_ref)):
        for c, piece in enumerate(_rope_rows(seg(j), pos, inv_ref[...])):
            ref[:, c * LANES:(c + 1) * LANES] = piece


def _in_proj_rows(x, g, w_bf, inv_lanes, period, offset):
    m = x.shape[0]
    tm = min(ROW_TILE, m)
    out = jax.ShapeDtypeStruct((m, ATT_WIDTH), F32)
    row_spec = pl.BlockSpec((tm, ATT_WIDTH), lambda i: (i, 0))
    return pl.pallas_call(
        functools.partial(_in_proj_rows_kernel, period=period, offset=offset),
        grid=(m // tm,),
        in_specs=[
            pl.BlockSpec((tm, D_MODEL), lambda i: (i, 0)),
            pl.BlockSpec((1, D_MODEL), lambda i: (0, 0)),
            pl.BlockSpec((D_MODEL, 4 * ATT_WIDTH), lambda i: (0, 0)),
            pl.BlockSpec((1, LANES), lambda i: (0, 0)),
        ],
        out_specs=[row_spec] * 4,
        out_shape=[out] * 4,
        compiler_params=_params(1, 40),
        name="in_proj_rows",
    )(x, g, w_bf, inv_lanes)


def _in_proj_prompt_kernel(x_ref, g_ref, wu_ref, wqkvt_ref, invc_ref, u_ref, qt_ref, kt_ref, vt_ref):
    tm = x_ref.shape[0]
    i = pl.program_id(1)
    n = _rmsnorm(x_ref[...], g_ref[...]).astype(BF16)
    u_ref[...] = _dot(n, wu_ref[...])
    vt_ref[0] = _dot_nt(wqkvt_ref[2 * ATT_WIDTH:, :], n)
    pos_t = (lax.broadcasted_iota(jnp.int32, (ROT_HALF, tm), 1) + i * tm).astype(F32)
    ang = invc_ref[...] * pos_t
    cos = jnp.cos(ang)
    sin = jnp.sin(ang)
    for off, ref in ((0, qt_ref), (ATT_WIDTH, kt_ref)):
        zt = _dot_nt(wqkvt_ref[off:off + ATT_WIDTH, :], n)
        for h in range(N_HEADS):
            r0 = h * HEAD_DIM
            x1 = zt[r0:r0 + ROT_HALF, :]
            x2 = zt[r0 + ROT_HALF:r0 + ROT_DIM, :]
            ref[0, r0:r0 + ROT_HALF, :] = x1 * cos - x2 * sin
            ref[0, r0 + ROT_HALF:r0 + ROT_DIM, :] = x2 * cos + x1 * sin
            ref[0, r0 + ROT_DIM:r0 + HEAD_DIM, :] = zt[r0 + ROT_DIM:r0 + HEAD_DIM, :]


def _in_proj_prompt(x, g, wu_bf, wqkvt_bf, inv_col, batch):
    tm = ROW_TILE
    tiles = SEQ // tm
    t_spec = pl.BlockSpec((1, ATT_WIDTH, tm), lambda b, i: (b, 0, i))
    feat = jax.ShapeDtypeStruct((batch, ATT_WIDTH, SEQ), F32)
    return pl.pallas_call(
        _in_proj_prompt_kernel,
        grid=(batch, tiles),
        in_specs=[
            pl.BlockSpec((tm, D_MODEL), lambda b, i: (b * tiles + i, 0)),
            pl.BlockSpec((1, D_MODEL), lambda b, i: (0, 0)),
            pl.BlockSpec((D_MODEL, POOL_WIDTH), lambda b, i: (0, 0)),
            pl.BlockSpec((3 * ATT_WIDTH, D_MODEL), lambda b, i: (0, 0)),
            pl.BlockSpec((ROT_HALF, 1), lambda b, i: (0, 0)),
        ],
        out_specs=[pl.BlockSpec((tm, POOL_WIDTH), lambda b, i: (b * tiles + i, 0)), t_spec, t_spec, t_spec],
        out_shape=[jax.ShapeDtypeStruct((batch * SEQ, POOL_WIDTH), F32), feat, feat, feat],
        compiler_params=_params(2, 40),
        name="in_proj_prompt",
    )(x, g, wu_bf, wqkvt_bf, inv_col)


def _pool_prompt_kernel(u_ref, prev_ref, w_ref, sc_ref, o_ref, ext_ref):
    tm = u_ref.shape[0]
    i = pl.program_id(0)
    pos0 = lax.rem(i * tm, SEQ)
    ext_ref[0:POOL_HALO, :] = jnp.where(pos0 == 0, 0.0, prev_ref[...])
    ext_ref[POOL_HALO:, :] = u_ref[...]
    pos = pos0 + lax.broadcasted_iota(jnp.int32, (tm, POOL_GW), 0)
    for g, w in enumerate(POOL_WINDOWS):
        cols = slice(g * POOL_GW, (g + 1) * POOL_GW)
        cur = u_ref[:, cols]
        s = cur
        for k in range(1, w):
            s = s + ext_ref[POOL_HALO - k:POOL_HALO - k + tm, cols]
        cnt = jnp.minimum(pos + 1, w).astype(F32)
        d = s / cnt - cur
        o_ref[:, cols] = _dot(d.astype(BF16), w_ref[g]) * sc_ref[:, cols]


def _pool_prompt(u, w_pool_bf, pool_scale):
    m = u.shape[0]
    tm = ROW_TILE
    halo_per_tile = tm // POOL_HALO
    return pl.pallas_call(
        _pool_prompt_kernel,
        grid=(m // tm,),
        in_specs=[
            pl.BlockSpec((tm, POOL_WIDTH), lambda i: (i, 0)),
            pl.BlockSpec((POOL_HALO, POOL_WIDTH), lambda i: (jnp.maximum(i * halo_per_tile - 1, 0), 0)),
            pl.BlockSpec((len(POOL_WINDOWS), POOL_GW, POOL_GW), lambda i: (0, 0, 0)),
            pl.BlockSpec((1, POOL_WIDTH), lambda i: (0, 0)),
        ],
        out_specs=pl.BlockSpec((tm, POOL_WIDTH), lambda i: (i, 0)),
        out_shape=jax.ShapeDtypeStruct((m, POOL_WIDTH), F32),
        scratch_shapes=[pltpu.VMEM((POOL_HALO + tm, POOL_WIDTH), F32)],
        compiler_params=_params(1, 32),
        name="pool_prompt",
    )(u, u, w_pool_bf, pool_scale)


def _pool_sample_kernel(e_ref, w_ref, sc_ref, o_ref):
    nb = e_ref.shape[0]
    lo = 1 + POOL_STATE
    t = lax.broadcasted_iota(jnp.int32, (nb, DEC_SEQ, POOL_GW), 1)
    for g, w in enumerate(POOL_WINDOWS):
        cols = slice(g * POOL_GW, (g + 1) * POOL_GW)
        cur = e_ref[:, lo:lo + DEC_SEQ, cols]
        s = cur
        for k in range(1, w):
            s = s + e_ref[:, lo - k:lo - k + DEC_SEQ, cols]
        cnt = jnp.minimum(PAST_LEN + t + 1, w).astype(F32)
        d = (s / cnt - cur).reshape(nb * DEC_SEQ, POOL_GW)
        o_ref[:, cols] = _dot(d.astype(BF16), w_ref[g]) * sc_ref[:, cols]


def _pool_sample(ext, w_pool_bf, pool_scale):
    db, rows, _ = ext.shape
    nb = 32
    return pl.pallas_call(
        _pool_sample_kernel,
        grid=(db // nb,),
        in_specs=[
            pl.BlockSpec((nb, rows, POOL_WIDTH), lambda i: (i, 0, 0)),
            pl.BlockSpec((len(POOL_WINDOWS), POOL_GW, POOL_GW), lambda i: (0, 0, 0)),
            pl.BlockSpec((1, POOL_WIDTH), lambda i: (0, 0)),
        ],
        out_specs=pl.BlockSpec((nb * DEC_SEQ, POOL_WIDTH), lambda i: (i, 0)),
        out_shape=jax.ShapeDtypeStruct((db * DEC_SEQ, POOL_WIDTH), F32),
        compiler_params=_params(1, 32),
        name="pool_sample",
    )(ext, w_pool_bf, pool_scale)


def _moba_prompt_kernel(qt_ref, kt_ref, vt_ref, o_ref, kaug_ref, kmean_ref):
    tq = MOBA_BLOCK
    blk = pl.program_id(2)
    n_blocks = SEQ // MOBA_BLOCK
    pad_rows = LANES - HEAD_DIM - n_blocks

    @pl.when(blk == 0)
    def _():
        block_row = lax.broadcasted_iota(jnp.int32, (n_blocks, MOBA_BLOCK), 0)
        lane = lax.broadcasted_iota(jnp.int32, (1, LANES), 1)
        for n in range(n_blocks):
            kb = kt_ref[0, :, n * MOBA_BLOCK:(n + 1) * MOBA_BLOCK]
            aug = jnp.concatenate([kb, jnp.where(block_row == n, 1.0, 0.0),
                                   jnp.zeros((pad_rows, MOBA_BLOCK), F32)], axis=0).T
            kaug_ref[n * MOBA_BLOCK:(n + 1) * MOBA_BLOCK, :] = aug.astype(BF16)
            kmean_ref[n:n + 1, :] = jnp.where(lane < HEAD_DIM,
                                               jnp.sum(aug, axis=0, keepdims=True) * (1.0 / MOBA_BLOCK), 0.0)

    qt = qt_ref[0]
    gate = jnp.dot(kmean_ref[...], jnp.concatenate([qt, jnp.zeros((LANES - HEAD_DIM, tq), F32)], axis=0),
                   precision=lax.Precision.HIGHEST, preferred_element_type=F32)
    n_idx = lax.broadcasted_iota(jnp.int32, (n_blocks, tq), 0)
    sel = _topk_mask(gate, n_idx < blk, axis=0)
    bias = jnp.where(sel > 0.5, 0.0, MASK_BIAS)
    q_log2 = qt * (HEAD_DIM ** -0.5 * LOG2_E)
    w_past = jnp.concatenate([q_log2, bias, jnp.zeros((pad_rows, tq), F32)], axis=0).astype(BF16)
    w_own = jnp.concatenate([q_log2, jnp.zeros((LANES - HEAD_DIM, tq), F32)], axis=0).astype(BF16)

    def scores(w, start, size):
        return _dot(kaug_ref[pl.ds(pl.multiple_of(start, MOBA_BLOCK), size), :], w)

    def values(start, size):
        vb = vt_ref[0, :, pl.ds(pl.multiple_of(start, MOBA_BLOCK), size)]
        return jnp.concatenate([vb.astype(BF16), jnp.ones((ONES_ROWS, size), BF16)], axis=0)

    key_pos = lax.broadcasted_iota(jnp.int32, (MOBA_BLOCK, tq), 0)
    q_pos = lax.broadcasted_iota(jnp.int32, (MOBA_BLOCK, tq), 1)
    s = jnp.where(key_pos <= q_pos, scores(w_own, blk * MOBA_BLOCK, MOBA_BLOCK), NEG_INF)
    m0 = jnp.max(s, axis=0, keepdims=True)
    acc0 = _dot(values(blk * MOBA_BLOCK, MOBA_BLOCK), jnp.exp2(s - m0).astype(BF16))

    span = PAST_GROUP * MOBA_BLOCK
    step = PAST_STEP * MOBA_BLOCK

    def body(i, carry):
        m, acc = carry
        starts = [i * span + u * step for u in range(PAST_GROUP // PAST_STEP)]
        tiles = [scores(w_past, start, step) for start in starts]
        for start, s in zip(starts, tiles):
            m_new = jnp.maximum(m, jnp.max(s, axis=0, keepdims=True))
            p = jnp.exp2(s - m_new).astype(BF16)
            acc = jnp.exp2(m - m_new) * acc + _dot(values(start, step), p)
            m = m_new
        return m, acc

    _, acc = lax.fori_loop(0, (blk + PAST_GROUP - 1) // PAST_GROUP, body, (m0, acc0))
    o_ref[0] = acc[0:HEAD_DIM, :] / acc[HEAD_DIM:HEAD_DIM + 1, :]


def _moba_prompt(qt, kt, vt):
    b = qt.shape[0]
    kv_spec = pl.BlockSpec((1, HEAD_DIM, SEQ), lambda bi, h, qi: (bi, h, 0))
    q_spec = pl.BlockSpec((1, HEAD_DIM, MOBA_BLOCK), lambda bi, h, qi: (bi, h, qi))
    return pl.pallas_call(
        _moba_prompt_kernel,
        grid=(b, N_HEADS, SEQ // MOBA_BLOCK),
        in_specs=[q_spec, kv_spec, kv_spec],
        out_specs=q_spec,
        out_shape=jax.ShapeDtypeStruct((b, ATT_WIDTH, SEQ), F32),
        scratch_shapes=[pltpu.VMEM((SEQ, LANES), BF16),
                        pltpu.VMEM((SEQ // MOBA_BLOCK, LANES), F32)],
        compiler_params=_params(3, 40),
        name="moba_prompt",
    )(qt, kt, vt)


def _moba_sample_kernel(pt_ref, q_ref, kn_ref, vn_ref, ck_hbm, cv_hbm, o_ref,
                        buf, sem, s_ref, p_ref):
    b = pl.program_id(0)
    n_seq = pl.num_programs(0)
    nb = N_PAST_BLOCKS
    n_rows = N_HEADS * DEC_SEQ
    scale = HEAD_DIM ** -0.5

    def page_copy(cache, seq, g, half):
        slot = lax.rem(g, KV_SLOTS)
        page = pt_ref[seq, lax.rem(g, nb) * PAGES_PER_BLOCK + half]
        return pltpu.make_async_copy(
            cache.at[page],
            buf.at[slot, :, pl.ds(half * PAGE_SIZE, PAGE_SIZE)],
            sem.at[slot, half])

    def start_chunk(seq, g):
        @pl.when(g < nb)
        def _():
            for half in range(PAGES_PER_BLOCK):
                page_copy(ck_hbm, seq, g, half).start()

        @pl.when(g >= nb)
        def _():
            for half in range(PAGES_PER_BLOCK):
                page_copy(cv_hbm, seq, g, half).start()

    def start_ahead(g):
        nxt = g + KV_SLOTS

        @pl.when(nxt < 2 * nb)
        def _():
            start_chunk(b, nxt)

        @pl.when((nxt >= 2 * nb) & (b + 1 < n_seq))
        def _():
            start_chunk(b + 1, nxt - 2 * nb)

    def wait_chunk(g):
        for half in range(PAGES_PER_BLOCK):
            page_copy(ck_hbm, b, g, half).wait()

    @pl.when(b == 0)
    def _():
        for g in range(KV_SLOTS):
            start_chunk(b, jnp.int32(g))

    q = q_ref[0]
    lane_head = lax.broadcasted_iota(jnp.int32, (DEC_SEQ, ATT_WIDTH), 1) // HEAD_DIM
    qbd = jnp.concatenate([jnp.where(lane_head == h, q, 0.0) for h in range(N_HEADS)], axis=0)
    q_hi = qbd.astype(BF16)
    q_lo = (qbd - q_hi.astype(F32)).astype(BF16)
    q_stack = jnp.concatenate([q_hi, q_lo], axis=0)

    def raw_scores(stacked):
        return stacked[0:n_rows] + stacked[n_rows:]

    gate_idx = lax.broadcasted_iota(jnp.int32, (n_rows, GATE_PAD), 1)

    def k_body(i, gate):
        chunks = [i * KV_UNROLL + u for u in range(KV_UNROLL)]
        for g in chunks:
            wait_chunk(g)
        for g in chunks:
            kb = buf[lax.rem(g, KV_SLOTS)].astype(BF16)
            s = raw_scores(_dot(q_stack, kb))
            s_ref[g] = s * scale
            gate = jnp.where(gate_idx == g, jnp.sum(s, axis=1, keepdims=True) * (1.0 / MOBA_BLOCK), gate)
        for g in chunks:
            start_ahead(g)
        return gate

    gate = lax.fori_loop(0, nb // KV_UNROLL, k_body, jnp.zeros((n_rows, GATE_PAD), F32))
    sel = _topk_mask(gate, gate_idx < nb, axis=1)

    pad = jnp.zeros((LANES - DEC_SEQ, ATT_WIDTH), F32)
    kn = jnp.concatenate([kn_ref[0], pad], axis=0).astype(BF16)
    vn = jnp.concatenate([vn_ref[0], pad], axis=0).astype(BF16)
    t_row = lax.rem(lax.broadcasted_iota(jnp.int32, (n_rows, LANES), 0), DEC_SEQ)
    t_col = lax.broadcasted_iota(jnp.int32, (n_rows, LANES), 1)
    s_own = jnp.where(t_col <= t_row, raw_scores(_dot_nt(q_stack, kn)) * scale, NEG_INF)

    m_lanes = jnp.full((n_rows, MOBA_BLOCK), NEG_INF, F32)
    for n in range(nb):
        m_lanes = jnp.maximum(m_lanes, jnp.where(sel[:, n:n + 1] > 0.5, s_ref[n], NEG_INF))
    m = jnp.maximum(jnp.max(s_own, axis=1, keepdims=True), jnp.max(m_lanes, axis=1, keepdims=True))
    p_own = jnp.exp(s_own - m)
    l_lanes = jnp.zeros((n_rows, MOBA_BLOCK), F32)
    for n in range(nb):
        pn = jnp.exp(jnp.where(sel[:, n:n + 1] > 0.5, s_ref[n], NEG_INF) - m)
        l_lanes = l_lanes + pn
        p_ref[n] = pn.astype(BF16)
    l = jnp.sum(p_own, axis=1, keepdims=True) + jnp.sum(l_lanes, axis=1, keepdims=True)

    def v_body(i, acc):
        chunks = [nb + i * KV_UNROLL + u for u in range(KV_UNROLL)]
        for g in chunks:
            wait_chunk(g)
        for g in chunks:
            vb = buf[lax.rem(g, KV_SLOTS)].astype(BF16)
            acc = acc + _dot_nt(p_ref[g - nb], vb)
        for g in chunks:
            start_ahead(g)
        return acc

    acc = lax.fori_loop(0, nb // KV_UNROLL, v_body, _dot(p_own.astype(BF16), vn))
    o = acc / l
    out = jnp.zeros((DEC_SEQ, ATT_WIDTH), F32)
    for h in range(N_HEADS):
        out = jnp.where(lane_head == h, o[h * DEC_SEQ:(h + 1) * DEC_SEQ, :], out)
    o_ref[0] = out


def _moba_sample(q, k_new, v_new, cache_kt, cache_vt, page_table):
    db = q.shape[0]
    n_rows = N_HEADS * DEC_SEQ
    tok_spec = pl.BlockSpec((1, DEC_SEQ, ATT_WIDTH), lambda b, pt: (b, 0, 0))
    grid_spec = pltpu.PrefetchScalarGridSpec(
        num_scalar_prefetch=1,
        grid=(db,),
        in_specs=[tok_spec, tok_spec, tok_spec,
                  pl.BlockSpec(memory_space=pl.ANY), pl.BlockSpec(memory_space=pl.ANY)],
        out_specs=tok_spec,
        scratch_shapes=[
            pltpu.VMEM((KV_SLOTS, ATT_WIDTH, MOBA_BLOCK), F32),
            pltpu.SemaphoreType.DMA((KV_SLOTS, PAGES_PER_BLOCK)),
            pltpu.VMEM((N_PAST_BLOCKS, n_rows, MOBA_BLOCK), F32),
            pltpu.VMEM((N_PAST_BLOCKS, n_rows, MOBA_BLOCK), BF16),
        ],
    )
    return pl.pallas_call(
        _moba_sample_kernel,
        grid_spec=grid_spec,
        out_shape=jax.ShapeDtypeStruct((db, DEC_SEQ, ATT_WIDTH), F32),
        compiler_params=_params(1, 40),
        name="moba_sample",
    )(page_table, q, k_new, v_new, cache_kt, cache_vt)


def _mix_out_kernel(a_ref, b_ref, w_ref, g_ref, r_ref, o_ref):
    att = b_ref[0].T if len(b_ref.shape) == 3 else b_ref[...]
    y = _dot(a_ref[...].astype(BF16), w_ref[0:POOL_WIDTH, :])
    y = y + _dot(att.astype(BF16), w_ref[POOL_WIDTH:, :])
    o_ref[...] = r_ref[...] + _rmsnorm(y, g_ref[...])


def _mix_out(pool, att, w_bf, g, res):
    m = res.shape[0]
    tm = min(ROW_TILE, m)
    half_spec = pl.BlockSpec((tm, POOL_WIDTH), lambda i: (i, 0))
    row_spec = pl.BlockSpec((tm, D_MODEL), lambda i: (i, 0))
    if att.ndim == 3:
        tiles = att.shape[2] // tm
        att_spec = pl.BlockSpec((1, ATT_WIDTH, tm), lambda i: (i // tiles, 0, i % tiles))
    else:
        att_spec = half_spec
    return pl.pallas_call(
        _mix_out_kernel,
        grid=(m // tm,),
        in_specs=[half_spec, att_spec,
                  pl.BlockSpec((D_MODEL, D_MODEL), lambda i: (0, 0)),
                  pl.BlockSpec((1, D_MODEL), lambda i: (0, 0)),
                  row_spec],
        out_specs=row_spec,
        out_shape=jax.ShapeDtypeStruct((m, D_MODEL), F32),
        compiler_params=_params(1, 32),
        name="mix_out",
    )(pool, att, w_bf, g, res)


def _mem_kv_kernel(x_ref, g_ref, wk_ref, wv_ref, k_ref, v_ref):
    n = _rmsnorm(x_ref[...], g_ref[...]).astype(BF16)
    k_ref[...] = _dot(n, wk_ref[...])
    v_ref[...] = _dot(n, wv_ref[...])


def _mem_kv(mem, g, wk_bf, wv_bf):
    m = mem.shape[0]
    tm = MEM_LEN
    row_spec = pl.BlockSpec((tm, D_MODEL), lambda i: (i, 0))
    w_spec = pl.BlockSpec((D_MODEL, D_MODEL), lambda i: (0, 0))
    out = jax.ShapeDtypeStruct((m, D_MODEL), F32)
    return pl.pallas_call(
        _mem_kv_kernel,
        grid=(m // tm,),
        in_specs=[row_spec, pl.BlockSpec((1, D_MODEL), lambda i: (0, 0)), w_spec, w_spec],
        out_specs=[row_spec, row_spec],
        out_shape=[out, out],
        compiler_params=_params(1, 32),
        name="mem_kv",
    )(mem, g, wk_bf, wv_bf)


def _xattn_kernel(h_ref, gpre_ref, wq_ref, mk_ref, mv_ref, wo_ref, gpost_ref, o_ref):
    tm = h_ref.shape[0]
    n_mem = mk_ref.shape[0]
    rows_per_mem = tm // n_mem
    scale = MEM_HEAD_DIM ** -0.5
    h = h_ref[...]
    q = _dot(_rmsnorm(h, gpre_ref[...]).astype(BF16), wq_ref[...])
    y = jnp.zeros((tm, D_MODEL), F32)
    for hh in range(MEM_HEADS):
        cols = slice(hh * MEM_HEAD_DIM, (hh + 1) * MEM_HEAD_DIM)
        qh = q[:, cols].astype(BF16).reshape(n_mem, rows_per_mem, MEM_HEAD_DIM)
        kh = mk_ref[:, :, cols].astype(BF16)
        vh = mv_ref[:, :, cols].astype(BF16)
        s = jnp.einsum("bqd,bkd->bqk", qh, kh, preferred_element_type=F32) * scale
        p = jnp.exp(s - jnp.max(s, axis=-1, keepdims=True))
        p = p / jnp.sum(p, axis=-1, keepdims=True)
        oh = jnp.einsum("bqk,bkd->bqd", p.astype(BF16), vh, preferred_element_type=F32)
        y = y + _dot(oh.reshape(tm, MEM_HEAD_DIM).astype(BF16), wo_ref[cols, :])
    o_ref[...] = h + _rmsnorm(y, gpost_ref[...])


def _xattn(h, gpre, wq_bf, mk, mv, wo_bf, gpost, rows_per_mem, tm):
    m = h.shape[0]
    row_spec = pl.BlockSpec((tm, D_MODEL), lambda i: (i, 0))
    vec_spec = pl.BlockSpec((1, D_MODEL), lambda i: (0, 0))
    w_spec = pl.BlockSpec((D_MODEL, D_MODEL), lambda i: (0, 0))
    if rows_per_mem >= tm:
        tiles_per_mem = rows_per_mem // tm
        mem_spec = pl.BlockSpec((1, MEM_LEN, D_MODEL), lambda i: (i // tiles_per_mem, 0, 0))
    else:
        mem_spec = pl.BlockSpec((tm // rows_per_mem, MEM_LEN, D_MODEL), lambda i: (i, 0, 0))
    return pl.pallas_call(
        _xattn_kernel,
        grid=(m // tm,),
        in_specs=[row_spec, vec_spec, w_spec, mem_spec, mem_spec, w_spec, vec_spec],
        out_specs=row_spec,
        out_shape=jax.ShapeDtypeStruct((m, D_MODEL), F32),
        compiler_params=_params(1, 48),
        name="xattn",
    )(h, gpre, wq_bf, mk, mv, wo_bf, gpost)


def _ffn_kernel(h_ref, gpre_ref, wu_ref, wd_ref, gpost_ref, o_ref):
    h = h_ref[...]
    n = _rmsnorm(h, gpre_ref[...]).astype(BF16)
    y = jnp.zeros(h.shape, F32)
    chunk = D_MODEL
    for c in range(D_FF // chunk):
        a = jnp.square(jnp.maximum(_dot(n, wu_ref[:, c * chunk:(c + 1) * chunk]), 0.0))
        y = y + _dot(a.astype(BF16), wd_ref[c * chunk:(c + 1) * chunk, :])
    o_ref[...] = h + _rmsnorm(y, gpost_ref[...])


def _ffn(h, gpre, wu_bf, wd_bf, gpost):
    m = h.shape[0]
    tm = min(ROW_TILE, m)
    row_spec = pl.BlockSpec((tm, D_MODEL), lambda i: (i, 0))
    vec_spec = pl.BlockSpec((1, D_MODEL), lambda i: (0, 0))
    return pl.pallas_call(
        _ffn_kernel,
        grid=(m // tm,),
        in_specs=[row_spec, vec_spec,
                  pl.BlockSpec((D_MODEL, D_FF), lambda i: (0, 0)),
                  pl.BlockSpec((D_FF, D_MODEL), lambda i: (0, 0)),
                  vec_spec],
        out_specs=row_spec,
        out_shape=jax.ShapeDtypeStruct((m, D_MODEL), F32),
        compiler_params=_params(1, 56),
        name="ffn",
    )(h, gpre, wu_bf, wd_bf, gpost)


def _rope_inv():
    inv = ROPE_THETA ** (-2.0 * jnp.arange(ROT_HALF, dtype=F32) / ROT_DIM)
    d = np.arange(LANES) % HEAD_DIM
    rotary = (d < ROT_DIM).astype(np.float32)
    inv_lanes = (inv[d % ROT_HALF] * rotary).reshape(1, LANES)
    return inv_lanes, inv.reshape(ROT_HALF, 1)


def kernel(x_prompt, x_sample, mem_prompt, cache_k, cache_v, cache_mem_k, cache_mem_v, state_pool, page_table,
           g_pre_mix, w_in, w_pool, pool_scale, w_out, g_post_mix,
           g_mem, g_pre_x, w_xq, w_xk, w_xv, w_xo, g_post_x,
           g_pre_ffn, w_up, w_down, g_post_ffn):
    depth = w_in.shape[0]
    batch = x_prompt.shape[0]
    db = x_sample.shape[0]
    n_phys = cache_k.shape[1]
    inv_lanes, inv_col = _rope_inv()

    hp = x_prompt.reshape(batch * SEQ, D_MODEL)
    hs = x_sample.reshape(db * DEC_SEQ, D_MODEL)
    pool_p, k_p, v_p, mk_p, mv_p, pool_s, k_s, v_s = ([] for _ in range(8))
    for l in range(depth):
        vec = lambda a: a[l].reshape(1, -1)
        w_in_bf = w_in[l].astype(BF16)
        w_u_bf = w_in_bf[:, :POOL_WIDTH]
        w_qkvt_bf = w_in_bf[:, POOL_WIDTH:].T
        w_pool_bf = w_pool[l].astype(BF16)
        w_out_bf = w_out[l].astype(BF16)
        w_xq_bf, w_xo_bf = w_xq[l].astype(BF16), w_xo[l].astype(BF16)
        w_up_bf, w_down_bf = w_up[l].astype(BF16), w_down[l].astype(BF16)

        u, qt, kt, vt = _in_proj_prompt(hp, vec(g_pre_mix), w_u_bf, w_qkvt_bf, inv_col, batch)
        pool = _pool_prompt(u, w_pool_bf, vec(pool_scale))
        hp = _mix_out(pool, _moba_prompt(qt, kt, vt), w_out_bf, vec(g_post_mix), hp)
        mk, mv = _mem_kv(mem_prompt.reshape(batch * MEM_LEN, D_MODEL), vec(g_mem),
                         w_xk[l].astype(BF16), w_xv[l].astype(BF16))
        hp = _xattn(hp, vec(g_pre_x), w_xq_bf, mk.reshape(batch, MEM_LEN, D_MODEL),
                    mv.reshape(batch, MEM_LEN, D_MODEL), w_xo_bf, vec(g_post_x),
                    rows_per_mem=SEQ, tm=ROW_TILE)
        hp = _ffn(hp, vec(g_pre_ffn), w_up_bf, w_down_bf, vec(g_post_ffn))
        pool_p.append(u.reshape(batch, SEQ, POOL_WIDTH)[:, SEQ - POOL_STATE:])
        k_p.append(kt.reshape(batch, N_HEADS, HEAD_DIM, SEQ).transpose(0, 3, 1, 2))
        v_p.append(vt.reshape(batch, N_HEADS, HEAD_DIM, SEQ).transpose(0, 3, 1, 2))
        mk_p.append(mk.reshape(batch, MEM_LEN, MEM_HEADS, MEM_HEAD_DIM))
        mv_p.append(mv.reshape(batch, MEM_LEN, MEM_HEADS, MEM_HEAD_DIM))

        u, q, k, v = _in_proj_rows(hs, vec(g_pre_mix), w_in_bf, inv_lanes, DEC_SEQ, PAST_LEN)
        u_ext = jnp.concatenate([state_pool[l], u.reshape(db, DEC_SEQ, POOL_WIDTH)], axis=1)
        pool = _pool_sample(jnp.pad(u_ext, ((0, 0), (1, 0), (0, 0))), w_pool_bf, vec(pool_scale))
        cache_kt = cache_k[l].transpose(0, 2, 3, 1).reshape(n_phys, ATT_WIDTH, PAGE_SIZE)
        cache_vt = cache_v[l].transpose(0, 2, 3, 1).reshape(n_phys, ATT_WIDTH, PAGE_SIZE)
        att = _moba_sample(q.reshape(db, DEC_SEQ, ATT_WIDTH), k.reshape(db, DEC_SEQ, ATT_WIDTH),
                           v.reshape(db, DEC_SEQ, ATT_WIDTH), cache_kt, cache_vt, page_table)
        hs = _mix_out(pool, att.reshape(db * DEC_SEQ, ATT_WIDTH), w_out_bf, vec(g_post_mix), hs)
        hs = _xattn(hs, vec(g_pre_x), w_xq_bf,
                    cache_mem_k[l].reshape(db, MEM_LEN, D_MODEL), cache_mem_v[l].reshape(db, MEM_LEN, D_MODEL),
                    w_xo_bf, vec(g_post_x), rows_per_mem=DEC_SEQ, tm=4 * DEC_SEQ)
        hs = _ffn(hs, vec(g_pre_ffn), w_up_bf, w_down_bf, vec(g_post_ffn))
        pool_s.append(u_ext[:, u_ext.shape[1] - POOL_STATE:])
        k_s.append(k.reshape(db, DEC_SEQ, N_HEADS, HEAD_DIM))
        v_s.append(v.reshape(db, DEC_SEQ, N_HEADS, HEAD_DIM))

    return (hp.reshape(batch, SEQ, D_MODEL), hs.reshape(db, DEC_SEQ, D_MODEL),
            jnp.stack(pool_p), jnp.stack(k_p), jnp.stack(v_p), jnp.stack(mk_p), jnp.stack(mv_p),
            jnp.stack(pool_s), jnp.stack(k_s), jnp.stack(v_s))
```

```python
import functools

import numpy as np
import jax
import jax.numpy as jnp
from jax import lax
from jax.experimental import pallas as pl
from jax.experimental.pallas import tpu as pltpu

F32 = jnp.float32
BF16 = jnp.bfloat16

D_MODEL = 1024
SEQ = 8192
DEC_SEQ = 8
PAST_LEN = 8192
PAGE_SIZE = 128
POOL_WIDTH = 512
POOL_WINDOWS = (2, 4, 8, 16)
POOL_GW = 128
POOL_STATE = 15
ATT_WIDTH = 512
N_HEADS = 8
HEAD_DIM = 64
ROT_DIM = 16
ROPE_THETA = 500000.0
MOBA_BLOCK = 256
MOBA_TOPK = 3
MEM_LEN = 256
MEM_HEADS = 4
MEM_HEAD_DIM = 256
D_FF = 4096
EPS = 1e-6

LANES = 128
SUBLANES = 8

ROW_TILE = 512
POOL_HALO = 16
N_PAST_BLOCKS = PAST_LEN // MOBA_BLOCK
PAGES_PER_BLOCK = MOBA_BLOCK // PAGE_SIZE
GATE_PAD = LANES
KV_SLOTS = 32
KV_UNROLL = 4
PAST_GROUP = 8
PAST_STEP = 2
PROMPT_HEADS = 2
ONES_ROWS = 16
ROT_HALF = ROT_DIM // 2
MEM_CHUNKS = MEM_HEAD_DIM // LANES
MEM_ROWS = MEM_CHUNKS * MEM_HEADS
NEG_INF = float("-inf")
MASK_BIAS = -1e30
LOG2_E = 1.4426950408889634

assert PAST_LEN % MOBA_BLOCK == 0, "sample own-block is assumed to hold new keys only"
assert PAST_LEN + 1 >= max(POOL_WINDOWS)
assert ROT_HALF == SUBLANES, "feature-major rotary assumes one sublane group per rotary half"
assert (2 * N_PAST_BLOCKS) % KV_SLOTS == 0 and N_PAST_BLOCKS % KV_UNROLL == 0 and KV_SLOTS % KV_UNROLL == 0
assert (SEQ // MOBA_BLOCK) % PAST_GROUP == 0 and HEAD_DIM + SEQ // MOBA_BLOCK <= LANES


def _params(n_grid, vmem_mib):
    return pltpu.CompilerParams(
        dimension_semantics=("arbitrary",) * n_grid,
        vmem_limit_bytes=vmem_mib * 1024 * 1024,
    )


def _rmsnorm(x, g):
    ms = jnp.mean(x * x, axis=-1, keepdims=True)
    return x * lax.rsqrt(ms + EPS) * g


def _dot(a, b):
    return jnp.dot(a, b, preferred_element_type=F32)


def _dot_nt(a, b, precision=None):
    return lax.dot_general(a, b, (((1,), (1,)), ((), ())), precision=precision,
                           preferred_element_type=F32)


def _topk_mask(gate, valid, axis):
    n = gate.shape[axis]
    idx = lax.broadcasted_iota(jnp.int32, gate.shape, axis).astype(F32)
    g = jnp.where(valid, gate, NEG_INF)
    sel = jnp.zeros(gate.shape, F32)
    for _ in range(MOBA_TOPK):
        mx = jnp.max(g, axis=axis, keepdims=True)
        first = jnp.min(jnp.where(g == mx, idx, float(n)), axis=axis, keepdims=True)
        pick = idx == first
        sel = jnp.where(pick, 1.0, sel)
        g = jnp.where(pick, NEG_INF, g)
    return jnp.where(valid, sel, 0.0)


def _rope_rows(z, pos_rows, inv_lanes):
    tm = z.shape[0]
    ang = pos_rows * inv_lanes
    cos = jnp.cos(ang)
    sin = jnp.sin(ang)
    d = lax.rem(lax.broadcasted_iota(jnp.int32, (tm, LANES), 1), HEAD_DIM)
    sin_lo = jnp.where(d < ROT_HALF, -sin, 0.0)
    sin_hi = jnp.where((d >= ROT_HALF) & (d < ROT_DIM), sin, 0.0)
    out = []
    for c in range(ATT_WIDTH // LANES):
        zz = z[:, c * LANES:(c + 1) * LANES]
        out.append(zz * cos
                   + pltpu.roll(zz, LANES - ROT_HALF, 1) * sin_lo
                   + pltpu.roll(zz, ROT_HALF, 1) * sin_hi)
    return out


def _in_proj_rows_kernel(x_ref, g_ref, w_ref, inv_ref, u_ref, q_ref, k_ref, v_ref, *, period, offset):
    tm = x_ref.shape[0]
    i = pl.program_id(0)
    n = _rmsnorm(x_ref[...], g_ref[...]).astype(BF16)

    def seg(j):
        return _dot(n, w_ref[:, j * ATT_WIDTH:(j + 1) * ATT_WIDTH])

    u_ref[...] = seg(0)
    v_ref[...] = seg(3)
    row = lax.broadcasted_iota(jnp.int32, (tm, LANES), 0) + i * tm
    pos = (lax.rem(row, period) + offset).astype(F32)
    for j, ref in ((1, q_ref), (2, k_ref)):
        for c, piece in enumerate(_rope_rows(seg(j), pos, inv_ref[...])):
            ref[:, c * LANES:(c + 1) * LANES] = piece


def _in_proj_rows(x, g, w_bf, inv_lanes, period, offset):
    m = x.shape[0]
    tm = min(ROW_TILE, m)
    out = jax.ShapeDtypeStruct((m, ATT_WIDTH), F32)
    row_spec = pl.BlockSpec((tm, ATT_WIDTH), lambda i: (i, 0))
    return pl.pallas_call(
        functools.partial(_in_proj_rows_kernel, period=period, offset=offset),
        grid=(m // tm,),
        in_specs=[
            pl.BlockSpec((tm, D_MODEL), lambda i: (i, 0)),
            pl.BlockSpec((1, D_MODEL), lambda i: (0, 0)),
            pl.BlockSpec((D_MODEL, 4 * ATT_WIDTH), lambda i: (0, 0)),
            pl.BlockSpec((1, LANES), lambda i: (0, 0)),
        ],
        out_specs=[row_spec] * 4,
        out_shape=[out] * 4,
        compiler_params=_params(1, 40),
        name="in_proj_rows",
    )(x, g, w_bf, inv_lanes)


def _in_proj_prompt_kernel(x_ref, g_ref, wu_ref, wqkvt_ref, invc_ref, u_ref, qt_ref, kt_ref, vt_ref):
    tm = x_ref.shape[0]
    i = pl.program_id(1)
    n = _rmsnorm(x_ref[...], g_ref[...]).astype(BF16)
    u_ref[...] = _dot(n, wu_ref[...])
    vt_ref[0] = _dot_nt(wqkvt_ref[2 * ATT_WIDTH:, :], n)
    pos_t = (lax.broadcasted_iota(jnp.int32, (ROT_HALF, tm), 1) + i * tm).astype(F32)
    ang = invc_ref[...] * pos_t
    cos = jnp.cos(ang)
    sin = jnp.sin(ang)
    for off, ref in ((0, qt_ref), (ATT_WIDTH, kt_ref)):
        zt = _dot_nt(wqkvt_ref[off:off + ATT_WIDTH, :], n)
        for h in range(N_HEADS):
            r0 = h * HEAD_DIM
            x1 = zt[r0:r0 + ROT_HALF, :]
            x2 = zt[r0 + ROT_HALF:r0 + ROT_DIM, :]
            ref[0, r0:r0 + ROT_HALF, :] = x1 * cos - x2 * sin
            ref[0, r0 + ROT_HALF:r0 + ROT_DIM, :] = x2 * cos + x1 * sin
            ref[0, r0 + ROT_DIM:r0 + HEAD_DIM, :] = zt[r0 + ROT_DIM:r0 + HEAD_DIM, :]


def _in_proj_prompt(x, g, wu_bf, wqkvt_bf, inv_col, batch):
    tm = ROW_TILE
    tiles = SEQ // tm
    t_spec = pl.BlockSpec((1, ATT_WIDTH, tm), lambda b, i: (b, 0, i))
    feat = jax.ShapeDtypeStruct((batch, ATT_WIDTH, SEQ), F32)
    return pl.pallas_call(
        _in_proj_prompt_kernel,
        grid=(batch, tiles),
        in_specs=[
            pl.BlockSpec((tm, D_MODEL), lambda b, i: (b * tiles + i, 0)),
            pl.BlockSpec((1, D_MODEL), lambda b, i: (0, 0)),
            pl.BlockSpec((D_MODEL, POOL_WIDTH), lambda b, i: (0, 0)),
            pl.BlockSpec((3 * ATT_WIDTH, D_MODEL), lambda b, i: (0, 0)),
            pl.BlockSpec((ROT_HALF, 1), lambda b, i: (0, 0)),
        ],
        out_specs=[pl.BlockSpec((tm, POOL_WIDTH), lambda b, i: (b * tiles + i, 0)), t_spec, t_spec, t_spec],
        out_shape=[jax.ShapeDtypeStruct((batch * SEQ, POOL_WIDTH), F32), feat, feat, feat],
        compiler_params=_params(2, 40),
        name="in_proj_prompt",
    )(x, g, wu_bf, wqkvt_bf, inv_col)


def _pool_prompt_kernel(u_ref, prev_ref, w_ref, sc_ref, o_ref, ext_ref):
    tm = u_ref.shape[0]
    i = pl.program_id(0)
    pos0 = lax.rem(i * tm, SEQ)
    ext_ref[0:POOL_HALO, :] = jnp.where(pos0 == 0, 0.0, prev_ref[...])
    ext_ref[POOL_HALO:, :] = u_ref[...]
    pos = pos0 + lax.broadcasted_iota(jnp.int32, (tm, POOL_GW), 0)
    for g, w in enumerate(POOL_WINDOWS):
        cols = slice(g * POOL_GW, (g + 1) * POOL_GW)
        cur = u_ref[:, cols]
        s = cur
        for k in range(1, w):
            s = s + ext_ref[POOL_HALO - k:POOL_HALO - k + tm, cols]
        cnt = jnp.minimum(pos + 1, w).astype(F32)
        d = s / cnt - cur
        o_ref[:, cols] = _dot(d.astype(BF16), w_ref[g]) * sc_ref[:, cols]


def _pool_prompt(u, w_pool_bf, pool_scale):
    m = u.shape[0]
    tm = ROW_TILE
    halo_per_tile = tm // POOL_HALO
    return pl.pallas_call(
        _pool_prompt_kernel,
        grid=(m // tm,),
        in_specs=[
            pl.BlockSpec((tm, POOL_WIDTH), lambda i: (i, 0)),
            pl.BlockSpec((POOL_HALO, POOL_WIDTH), lambda i: (jnp.maximum(i * halo_per_tile - 1, 0), 0)),
            pl.BlockSpec((len(POOL_WINDOWS), POOL_GW, POOL_GW), lambda i: (0, 0, 0)),
            pl.BlockSpec((1, POOL_WIDTH), lambda i: (0, 0)),
        ],
        out_specs=pl.BlockSpec((tm, POOL_WIDTH), lambda i: (i, 0)),
        out_shape=jax.ShapeDtypeStruct((m, POOL_WIDTH), F32),
        scratch_shapes=[pltpu.VMEM((POOL_HALO + tm, POOL_WIDTH), F32)],
        compiler_params=_params(1, 32),
        name="pool_prompt",
    )(u, u, w_pool_bf, pool_scale)


def _pool_sample_kernel(e_ref, w_ref, sc_ref, o_ref):
    nb = e_ref.shape[0]
    lo = 1 + POOL_STATE
    t = lax.broadcasted_iota(jnp.int32, (nb, DEC_SEQ, POOL_GW), 1)
    for g, w in enumerate(POOL_WINDOWS):
        cols = slice(g * POOL_GW, (g + 1) * POOL_GW)
        cur = e_ref[:, lo:lo + DEC_SEQ, cols]
        s = cur
        for k in range(1, w):
            s = s + e_ref[:, lo - k:lo - k + DEC_SEQ, cols]
        cnt = jnp.minimum(PAST_LEN + t + 1, w).astype(F32)
        d = (s / cnt - cur).reshape(nb * DEC_SEQ, POOL_GW)
        o_ref[:, cols] = _dot(d.astype(BF16), w_ref[g]) * sc_ref[:, cols]


def _pool_sample(ext, w_pool_bf, pool_scale):
    db, rows, _ = ext.shape
    nb = 32
    return pl.pallas_call(
        _pool_sample_kernel,
        grid=(db // nb,),
        in_specs=[
            pl.BlockSpec((nb, rows, POOL_WIDTH), lambda i: (i, 0, 0)),
            pl.BlockSpec((len(POOL_WINDOWS), POOL_GW, POOL_GW), lambda i: (0, 0, 0)),
            pl.BlockSpec((1, POOL_WIDTH), lambda i: (0, 0)),
        ],
        out_specs=pl.BlockSpec((nb * DEC_SEQ, POOL_WIDTH), lambda i: (i, 0)),
        out_shape=jax.ShapeDtypeStruct((db * DEC_SEQ, POOL_WIDTH), F32),
        compiler_params=_params(1, 32),
        name="pool_sample",
    )(ext, w_pool_bf, pool_scale)


def _moba_prompt_kernel(qt_ref, kt_ref, vt_ref, o_ref, kaug_ref, kmean_ref):
    tq = MOBA_BLOCK
    blk = pl.program_id(2)
    n_blocks = SEQ // MOBA_BLOCK
    pad_rows = LANES - HEAD_DIM - n_blocks

    heads = range(PROMPT_HEADS)
    head_rows = [slice(a * HEAD_DIM, (a + 1) * HEAD_DIM) for a in heads]

    @pl.when(blk == 0)
    def _():
        block_row = lax.broadcasted_iota(jnp.int32, (n_blocks, MOBA_BLOCK), 0)
        lane = lax.broadcasted_iota(jnp.int32, (1, LANES), 1)
        for a in heads:
            for n in range(n_blocks):
                kb = kt_ref[0, head_rows[a], n * MOBA_BLOCK:(n + 1) * MOBA_BLOCK]
                aug = jnp.concatenate([kb, jnp.where(block_row == n, 1.0, 0.0),
                                       jnp.zeros((pad_rows, MOBA_BLOCK), F32)], axis=0).T
                kaug_ref[a, n * MOBA_BLOCK:(n + 1) * MOBA_BLOCK, :] = aug.astype(BF16)
                kmean_ref[a, n:n + 1, :] = jnp.where(
                    lane < HEAD_DIM, jnp.sum(aug, axis=0, keepdims=True) * (1.0 / MOBA_BLOCK), 0.0)

    n_idx = lax.broadcasted_iota(jnp.int32, (n_blocks, tq), 0)
    w_past, w_own = [], []
    for a in heads:
        qt = qt_ref[0, head_rows[a], :]
        gate = jnp.dot(kmean_ref[a], jnp.concatenate([qt, jnp.zeros((LANES - HEAD_DIM, tq), F32)], axis=0),
                       precision=lax.Precision.HIGHEST, preferred_element_type=F32)
        sel = _topk_mask(gate, n_idx < blk, axis=0)
        bias = jnp.where(sel > 0.5, 0.0, MASK_BIAS)
        q_log2 = qt * (HEAD_DIM ** -0.5 * LOG2_E)
        w_past.append(jnp.concatenate([q_log2, bias, jnp.zeros((pad_rows, tq), F32)], axis=0).astype(BF16))
        w_own.append(jnp.concatenate([q_log2, jnp.zeros((LANES - HEAD_DIM, tq), F32)], axis=0).astype(BF16))

    def scores(a, w, start, size):
        return _dot(kaug_ref[a, pl.ds(pl.multiple_of(start, MOBA_BLOCK), size), :], w)

    def values(a, start, size):
        vb = vt_ref[0, head_rows[a], pl.ds(pl.multiple_of(start, MOBA_BLOCK), size)]
        return jnp.concatenate([vb.astype(BF16), jnp.ones((ONES_ROWS, size), BF16)], axis=0)

    key_pos = lax.broadcasted_iota(jnp.int32, (MOBA_BLOCK, tq), 0)
    q_pos = lax.broadcasted_iota(jnp.int32, (MOBA_BLOCK, tq), 1)
    own = [jnp.where(key_pos <= q_pos, scores(a, w_own[a], blk * MOBA_BLOCK, MOBA_BLOCK), NEG_INF) for a in heads]
    init = []
    for a in heads:
        m0 = jnp.max(own[a], axis=0, keepdims=True)
        init += [m0, _dot(values(a, blk * MOBA_BLOCK, MOBA_BLOCK), jnp.exp2(own[a] - m0).astype(BF16))]

    span = PAST_GROUP * MOBA_BLOCK
    step = PAST_STEP * MOBA_BLOCK

    def body(i, carry):
        carry = list(carry)
        starts = [i * span + u * step for u in range(PAST_GROUP // PAST_STEP)]
        tiles = [[scores(a, w_past[a], start, step) for start in starts] for a in heads]
        for u, start in enumerate(starts):
            for a in heads:
                m, acc = carry[2 * a:2 * a + 2]
                s = tiles[a][u]
                m_new = jnp.maximum(m, jnp.max(s, axis=0, keepdims=True))
                p = jnp.exp2(s - m_new).astype(BF16)
                carry[2 * a:2 * a + 2] = [m_new, jnp.exp2(m - m_new) * acc + _dot(values(a, start, step), p)]
        return tuple(carry)

    res = lax.fori_loop(0, (blk + PAST_GROUP - 1) // PAST_GROUP, body, tuple(init))
    for a in heads:
        acc = res[2 * a + 1]
        o_ref[0, head_rows[a], :] = acc[0:HEAD_DIM, :] / acc[HEAD_DIM:HEAD_DIM + 1, :]


def _moba_prompt(qt, kt, vt):
    b = qt.shape[0]
    rows = PROMPT_HEADS * HEAD_DIM
    kv_spec = pl.BlockSpec((1, rows, SEQ), lambda bi, h, qi: (bi, h, 0))
    q_spec = pl.BlockSpec((1, rows, MOBA_BLOCK), lambda bi, h, qi: (bi, h, qi))
    return pl.pallas_call(
        _moba_prompt_kernel,
        grid=(b, N_HEADS // PROMPT_HEADS, SEQ // MOBA_BLOCK),
        in_specs=[q_spec, kv_spec, kv_spec],
        out_specs=q_spec,
        out_shape=jax.ShapeDtypeStruct((b, ATT_WIDTH, SEQ), F32),
        scratch_shapes=[pltpu.VMEM((PROMPT_HEADS, SEQ, LANES), BF16),
                        pltpu.VMEM((PROMPT_HEADS, SEQ // MOBA_BLOCK, LANES), F32)],
        compiler_params=_params(3, 48),
        name="moba_prompt",
    )(qt, kt, vt)


def _moba_sample_kernel(pt_ref, q_ref, kn_ref, vn_ref, ck_hbm, cv_hbm, o_ref,
                        buf, sem, s_ref, p_ref):
    b = pl.program_id(0)
    n_seq = pl.num_programs(0)
    nb = N_PAST_BLOCKS
    n_rows = N_HEADS * DEC_SEQ
    scale = HEAD_DIM ** -0.5

    def page_copy(cache, seq, g, half):
        slot = lax.rem(g, KV_SLOTS)
        page = pt_ref[seq, lax.rem(g, nb) * PAGES_PER_BLOCK + half]
        return pltpu.make_async_copy(
            cache.at[page],
            buf.at[slot, :, pl.ds(half * PAGE_SIZE, PAGE_SIZE)],
            sem.at[slot, half])

    def start_chunk(seq, g):
        @pl.when(g < nb)
        def _():
            for half in range(PAGES_PER_BLOCK):
                page_copy(ck_hbm, seq, g, half).start()

        @pl.when(g >= nb)
        def _():
            for half in range(PAGES_PER_BLOCK):
                page_copy(cv_hbm, seq, g, half).start()

    def start_ahead(g):
        nxt = g + KV_SLOTS

        @pl.when(nxt < 2 * nb)
        def _():
            start_chunk(b, nxt)

        @pl.when((nxt >= 2 * nb) & (b + 1 < n_seq))
        def _():
            start_chunk(b + 1, nxt - 2 * nb)

    def wait_chunk(g):
        for half in range(PAGES_PER_BLOCK):
            page_copy(ck_hbm, b, g, half).wait()

    @pl.when(b == 0)
    def _():
        for g in range(KV_SLOTS):
            start_chunk(b, jnp.int32(g))

    q = q_ref[0]
    lane_head = lax.broadcasted_iota(jnp.int32, (DEC_SEQ, ATT_WIDTH), 1) // HEAD_DIM
    qbd = jnp.concatenate([jnp.where(lane_head == h, q, 0.0) for h in range(N_HEADS)], axis=0)
    q_hi = qbd.astype(BF16)
    q_lo = (qbd - q_hi.astype(F32)).astype(BF16)
    q_stack = jnp.concatenate([q_hi, q_lo], axis=0)

    def raw_scores(stacked):
        return stacked[0:n_rows] + stacked[n_rows:]

    gate_idx = lax.broadcasted_iota(jnp.int32, (n_rows, GATE_PAD), 1)

    def k_body(i, gate):
        chunks = [i * KV_UNROLL + u for u in range(KV_UNROLL)]
        for g in chunks:
            wait_chunk(g)
        for g in chunks:
            kb = buf[lax.rem(g, KV_SLOTS)].astype(BF16)
            s = raw_scores(_dot(q_stack, kb))
            s_ref[g] = s * scale
            gate = jnp.where(gate_idx == g, jnp.sum(s, axis=1, keepdims=True) * (1.0 / MOBA_BLOCK), gate)
        for g in chunks:
            start_ahead(g)
        return gate

    gate = lax.fori_loop(0, nb // KV_UNROLL, k_body, jnp.zeros((n_rows, GATE_PAD), F32))
    sel = _topk_mask(gate, gate_idx < nb, axis=1)

    pad = jnp.zeros((LANES - DEC_SEQ, ATT_WIDTH), F32)
    kn = jnp.concatenate([kn_ref[0], pad], axis=0).astype(BF16)
    vn = jnp.concatenate([vn_ref[0], pad], axis=0).astype(BF16)
    t_row = lax.rem(lax.broadcasted_iota(jnp.int32, (n_rows, LANES), 0), DEC_SEQ)
    t_col = lax.broadcasted_iota(jnp.int32, (n_rows, LANES), 1)
    s_own = jnp.where(t_col <= t_row, raw_scores(_dot_nt(q_stack, kn)) * scale, NEG_INF)

    m_lanes = jnp.full((n_rows, MOBA_BLOCK), NEG_INF, F32)
    for n in range(nb):
        m_lanes = jnp.maximum(m_lanes, jnp.where(sel[:, n:n + 1] > 0.5, s_ref[n], NEG_INF))
    m = jnp.maximum(jnp.max(s_own, axis=1, keepdims=True), jnp.max(m_lanes, axis=1, keepdims=True))
    p_own = jnp.exp(s_own - m)
    l_lanes = jnp.zeros((n_rows, MOBA_BLOCK), F32)
    for n in range(nb):
        pn = jnp.exp(jnp.where(sel[:, n:n + 1] > 0.5, s_ref[n], NEG_INF) - m)
        l_lanes = l_lanes + pn
        p_ref[n] = pn.astype(BF16)
    l = jnp.sum(p_own, axis=1, keepdims=True) + jnp.sum(l_lanes, axis=1, keepdims=True)

    def v_body(i, acc):
        chunks = [nb + i * KV_UNROLL + u for u in range(KV_UNROLL)]
        for g in chunks:
            wait_chunk(g)
        for g in chunks:
            vb = buf[lax.rem(g, KV_SLOTS)].astype(BF16)
            acc = acc + _dot_nt(p_ref[g - nb], vb)
        for g in chunks:
            start_ahead(g)
        return acc

    acc = lax.fori_loop(0, nb // KV_UNROLL, v_body, _dot(p_own.astype(BF16), vn))
    o = acc / l
    out = jnp.zeros((DEC_SEQ, ATT_WIDTH), F32)
    for h in range(N_HEADS):
        out = jnp.where(lane_head == h, o[h * DEC_SEQ:(h + 1) * DEC_SEQ, :], out)
    o_ref[0] = out


def _moba_sample(q, k_new, v_new, cache_kt, cache_vt, page_table):
    db = q.shape[0]
    n_rows = N_HEADS * DEC_SEQ
    tok_spec = pl.BlockSpec((1, DEC_SEQ, ATT_WIDTH), lambda b, pt: (b, 0, 0))
    grid_spec = pltpu.PrefetchScalarGridSpec(
        num_scalar_prefetch=1,
        grid=(db,),
        in_specs=[tok_spec, tok_spec, tok_spec,
                  pl.BlockSpec(memory_space=pl.ANY), pl.BlockSpec(memory_space=pl.ANY)],
        out_specs=tok_spec,
        scratch_shapes=[
            pltpu.VMEM((KV_SLOTS, ATT_WIDTH, MOBA_BLOCK), F32),
            pltpu.SemaphoreType.DMA((KV_SLOTS, PAGES_PER_BLOCK)),
            pltpu.VMEM((N_PAST_BLOCKS, n_rows, MOBA_BLOCK), F32),
            pltpu.VMEM((N_PAST_BLOCKS, n_rows, MOBA_BLOCK), BF16),
        ],
    )
    return pl.pallas_call(
        _moba_sample_kernel,
        grid_spec=grid_spec,
        out_shape=jax.ShapeDtypeStruct((db, DEC_SEQ, ATT_WIDTH), F32),
        compiler_params=_params(1, 40),
        name="moba_sample",
    )(page_table, q, k_new, v_new, cache_kt, cache_vt)


def _mix_out_kernel(a_ref, b_ref, w_ref, g_ref, r_ref, o_ref):
    att = b_ref[0].T if len(b_ref.shape) == 3 else b_ref[...]
    y = _dot(a_ref[...].astype(BF16), w_ref[0:POOL_WIDTH, :])
    y = y + _dot(att.astype(BF16), w_ref[POOL_WIDTH:, :])
    o_ref[...] = r_ref[...] + _rmsnorm(y, g_ref[...])


def _mix_out(pool, att, w_bf, g, res):
    m = res.shape[0]
    tm = min(ROW_TILE, m)
    half_spec = pl.BlockSpec((tm, POOL_WIDTH), lambda i: (i, 0))
    row_spec = pl.BlockSpec((tm, D_MODEL), lambda i: (i, 0))
    if att.ndim == 3:
        tiles = att.shape[2] // tm
        att_spec = pl.BlockSpec((1, ATT_WIDTH, tm), lambda i: (i // tiles, 0, i % tiles))
    else:
        att_spec = half_spec
    return pl.pallas_call(
        _mix_out_kernel,
        grid=(m // tm,),
        in_specs=[half_spec, att_spec,
                  pl.BlockSpec((D_MODEL, D_MODEL), lambda i: (0, 0)),
                  pl.BlockSpec((1, D_MODEL), lambda i: (0, 0)),
                  row_spec],
        out_specs=row_spec,
        out_shape=jax.ShapeDtypeStruct((m, D_MODEL), F32),
        compiler_params=_params(1, 32),
        name="mix_out",
    )(pool, att, w_bf, g, res)


def _mem_kv_kernel(x_ref, g_ref, wk_ref, wv_ref, k_ref, v_ref):
    n = _rmsnorm(x_ref[...], g_ref[...]).astype(BF16)
    k_ref[...] = _dot(n, wk_ref[...])
    v_ref[...] = _dot(n, wv_ref[...])


def _mem_kv(mem, g, wk_bf, wv_bf):
    m = mem.shape[0]
    tm = MEM_LEN
    row_spec = pl.BlockSpec((tm, D_MODEL), lambda i: (i, 0))
    w_spec = pl.BlockSpec((D_MODEL, D_MODEL), lambda i: (0, 0))
    out = jax.ShapeDtypeStruct((m, D_MODEL), F32)
    return pl.pallas_call(
        _mem_kv_kernel,
        grid=(m // tm,),
        in_specs=[row_spec, pl.BlockSpec((1, D_MODEL), lambda i: (0, 0)), w_spec, w_spec],
        out_specs=[row_spec, row_spec],
        out_shape=[out, out],
        compiler_params=_params(1, 32),
        name="mem_kv",
    )(mem, g, wk_bf, wv_bf)


def _xattn_kernel(h_ref, gpre_ref, wq_ref, mk_ref, mv_ref, wo_ref, gpost_ref, o_ref):
    tm = h_ref.shape[0]
    scale = MEM_HEAD_DIM ** -0.5
    h = h_ref[...]
    q = _dot(_rmsnorm(h, gpre_ref[...]).astype(BF16), wq_ref[...]).astype(BF16)
    y = jnp.zeros((tm, D_MODEL), F32)
    for hh in range(MEM_HEADS):
        cols = slice(hh * MEM_HEAD_DIM, (hh + 1) * MEM_HEAD_DIM)
        s = _dot_nt(q[:, cols], mk_ref[0, :, cols].astype(BF16)) * scale
        p = jnp.exp(s - jnp.max(s, axis=-1, keepdims=True))
        p = (p / jnp.sum(p, axis=-1, keepdims=True)).astype(BF16)
        oh = _dot(p, mv_ref[0, :, cols].astype(BF16))
        y = y + _dot(oh.astype(BF16), wo_ref[cols, :])
    o_ref[...] = h + _rmsnorm(y, gpost_ref[...])


def _xattn_sample_kernel(h_ref, gpre_ref, wq_ref, mk_ref, mv_ref, wo_ref, gpost_ref, o_ref):
    tm = h_ref.shape[0]
    n_mem = mk_ref.shape[0]
    t_rows = tm // n_mem
    width = MEM_LEN * MEM_ROWS
    n_q = MEM_HEADS * t_rows
    scale = MEM_HEAD_DIM ** -0.5
    h = h_ref[...]
    q = _dot(_rmsnorm(h, gpre_ref[...]).astype(BF16), wq_ref[...]).astype(BF16)
    col = lax.broadcasted_iota(jnp.int32, (n_q, width), 1)
    row_head = lax.broadcasted_iota(jnp.int32, (n_q, width), 0) // t_rows
    own = (lax.rem(col, MEM_ROWS) == row_head)
    rows = []
    for b in range(n_mem):
        keys = mk_ref[b].reshape(width, LANES).astype(BF16)
        vals = mv_ref[b].reshape(width, LANES).astype(BF16)
        qb = q[b * t_rows:(b + 1) * t_rows, :]
        s = None
        for c in range(MEM_CHUNKS):
            qc = jnp.concatenate([qb[:, hh * MEM_HEAD_DIM + c * LANES:hh * MEM_HEAD_DIM + (c + 1) * LANES]
                                  for hh in range(MEM_HEADS)], axis=0)
            sc = _dot_nt(qc, keys)
            if c:
                sc = pltpu.roll(sc, width - c * MEM_HEADS, 1)
            s = sc if s is None else s + sc
        s = jnp.where(own, s * scale, NEG_INF)
        p = jnp.exp(s - jnp.max(s, axis=-1, keepdims=True))
        p = p / jnp.sum(p, axis=-1, keepdims=True)
        pieces = [[None] * MEM_CHUNKS for _ in range(MEM_HEADS)]
        for c in range(MEM_CHUNKS):
            pc = pltpu.roll(p, c * MEM_HEADS, 1) if c else p
            oc = _dot(pc.astype(BF16), vals)
            for hh in range(MEM_HEADS):
                pieces[hh][c] = oc[hh * t_rows:(hh + 1) * t_rows, :]
        rows.append(jnp.concatenate([pc for head in pieces for pc in head], axis=1))
    o = jnp.concatenate(rows, axis=0)
    o_ref[...] = h + _rmsnorm(_dot(o.astype(BF16), wo_ref[...]), gpost_ref[...])


def _xattn_sample(h, gpre, wq_bf, mk, mv, wo_bf, gpost, n_mem):
    m = h.shape[0]
    tm = n_mem * (m // mk.shape[0])
    row_spec = pl.BlockSpec((tm, D_MODEL), lambda i: (i, 0))
    vec_spec = pl.BlockSpec((1, D_MODEL), lambda i: (0, 0))
    w_spec = pl.BlockSpec((D_MODEL, D_MODEL), lambda i: (0, 0))
    mem_spec = pl.BlockSpec((n_mem, MEM_LEN, MEM_ROWS, LANES), lambda i: (i, 0, 0, 0))
    return pl.pallas_call(
        _xattn_sample_kernel,
        grid=(m // tm,),
        in_specs=[row_spec, vec_spec, w_spec, mem_spec, mem_spec, w_spec, vec_spec],
        out_specs=row_spec,
        out_shape=jax.ShapeDtypeStruct((m, D_MODEL), F32),
        compiler_params=_params(1, 48),
        name="xattn_sample",
    )(h, gpre, wq_bf, mk, mv, wo_bf, gpost)


def _xattn(h, gpre, wq_bf, mk, mv, wo_bf, gpost, rows_per_mem, tm):
    m = h.shape[0]
    row_spec = pl.BlockSpec((tm, D_MODEL), lambda i: (i, 0))
    vec_spec = pl.BlockSpec((1, D_MODEL), lambda i: (0, 0))
    w_spec = pl.BlockSpec((D_MODEL, D_MODEL), lambda i: (0, 0))
    tiles_per_mem = rows_per_mem // tm
    mem_spec = pl.BlockSpec((1, MEM_LEN, D_MODEL), lambda i: (i // tiles_per_mem, 0, 0))
    return pl.pallas_call(
        _xattn_kernel,
        grid=(m // tm,),
        in_specs=[row_spec, vec_spec, w_spec, mem_spec, mem_spec, w_spec, vec_spec],
        out_specs=row_spec,
        out_shape=jax.ShapeDtypeStruct((m, D_MODEL), F32),
        compiler_params=_params(1, 48),
        name="xattn",
    )(h, gpre, wq_bf, mk, mv, wo_bf, gpost)


def _ffn_kernel(h_ref, gpre_ref, wu_ref, wd_ref, gpost_ref, o_ref):
    h = h_ref[...]
    n = _rmsnorm(h, gpre_ref[...]).astype(BF16)
    y = jnp.zeros(h.shape, F32)
    chunk = D_MODEL
    for c in range(D_FF // chunk):
        a = jnp.square(jnp.maximum(_dot(n, wu_ref[:, c * chunk:(c + 1) * chunk]), 0.0))
        y = y + _dot(a.astype(BF16), wd_ref[c * chunk:(c + 1) * chunk, :])
    o_ref[...] = h + _rmsnorm(y, gpost_ref[...])


def _ffn(h, gpre, wu_bf, wd_bf, gpost):
    m = h.shape[0]
    tm = min(ROW_TILE, m)
    row_spec = pl.BlockSpec((tm, D_MODEL), lambda i: (i, 0))
    vec_spec = pl.BlockSpec((1, D_MODEL), lambda i: (0, 0))
    return pl.pallas_call(
        _ffn_kernel,
        grid=(m // tm,),
        in_specs=[row_spec, vec_spec,
                  pl.BlockSpec((D_MODEL, D_FF), lambda i: (0, 0)),
                  pl.BlockSpec((D_FF, D_MODEL), lambda i: (0, 0)),
                  vec_spec],
        out_specs=row_spec,
        out_shape=jax.ShapeDtypeStruct((m, D_MODEL), F32),
        compiler_params=_params(1, 56),
        name="ffn",
    )(h, gpre, wu_bf, wd_bf, gpost)


def _rope_inv():
    inv = ROPE_THETA ** (-2.0 * jnp.arange(ROT_HALF, dtype=F32) / ROT_DIM)
    d = np.arange(LANES) % HEAD_DIM
    rotary = (d < ROT_DIM).astype(np.float32)
    inv_lanes = (inv[d % ROT_HALF] * rotary).reshape(1, LANES)
    return inv_lanes, inv.reshape(ROT_HALF, 1)


def kernel(x_prompt, x_sample, mem_prompt, cache_k, cache_v, cache_mem_k, cache_mem_v, state_pool, page_table,
           g_pre_mix, w_in, w_pool, pool_scale, w_out, g_post_mix,
           g_mem, g_pre_x, w_xq, w_xk, w_xv, w_xo, g_post_x,
           g_pre_ffn, w_up, w_down, g_post_ffn):
    depth = w_in.shape[0]
    batch = x_prompt.shape[0]
    db = x_sample.shape[0]
    n_phys = cache_k.shape[1]
    inv_lanes, inv_col = _rope_inv()

    hp = x_prompt.reshape(batch * SEQ, D_MODEL)
    hs = x_sample.reshape(db * DEC_SEQ, D_MODEL)
    pool_p, k_p, v_p, mk_p, mv_p, pool_s, k_s, v_s = ([] for _ in range(8))
    for l in range(depth):
        vec = lambda a: a[l].reshape(1, -1)
        w_in_bf = w_in[l].astype(BF16)
        w_u_bf = w_in_bf[:, :POOL_WIDTH]
        w_qkvt_bf = w_in_bf[:, POOL_WIDTH:].T
        w_pool_bf = w_pool[l].astype(BF16)
        w_out_bf = w_out[l].astype(BF16)
        w_xq_bf, w_xo_bf = w_xq[l].astype(BF16), w_xo[l].astype(BF16)
        w_up_bf, w_down_bf = w_up[l].astype(BF16), w_down[l].astype(BF16)

        u, qt, kt, vt = _in_proj_prompt(hp, vec(g_pre_mix), w_u_bf, w_qkvt_bf, inv_col, batch)
        pool = _pool_prompt(u, w_pool_bf, vec(pool_scale))
        hp = _mix_out(pool, _moba_prompt(qt, kt, vt), w_out_bf, vec(g_post_mix), hp)
        mk, mv = _mem_kv(mem_prompt.reshape(batch * MEM_LEN, D_MODEL), vec(g_mem),
                         w_xk[l].astype(BF16), w_xv[l].astype(BF16))
        hp = _xattn(hp, vec(g_pre_x), w_xq_bf, mk.reshape(batch, MEM_LEN, D_MODEL),
                    mv.reshape(batch, MEM_LEN, D_MODEL), w_xo_bf, vec(g_post_x),
                    rows_per_mem=SEQ, tm=ROW_TILE)
        hp = _ffn(hp, vec(g_pre_ffn), w_up_bf, w_down_bf, vec(g_post_ffn))
        pool_p.append(u.reshape(batch, SEQ, POOL_WIDTH)[:, SEQ - POOL_STATE:])
        k_p.append(kt.reshape(batch, N_HEADS, HEAD_DIM, SEQ).transpose(0, 3, 1, 2))
        v_p.append(vt.reshape(batch, N_HEADS, HEAD_DIM, SEQ).transpose(0, 3, 1, 2))
        mk_p.append(mk.reshape(batch, MEM_LEN, MEM_HEADS, MEM_HEAD_DIM))
        mv_p.append(mv.reshape(batch, MEM_LEN, MEM_HEADS, MEM_HEAD_DIM))

        u, q, k, v = _in_proj_rows(hs, vec(g_pre_mix), w_in_bf, inv_lanes, DEC_SEQ, PAST_LEN)
        u_ext = jnp.concatenate([state_pool[l], u.reshape(db, DEC_SEQ, POOL_WIDTH)], axis=1)
        pool = _pool_sample(jnp.pad(u_ext, ((0, 0), (1, 0), (0, 0))), w_pool_bf, vec(pool_scale))
        cache_kt = cache_k[l].transpose(0, 2, 3, 1).reshape(n_phys, ATT_WIDTH, PAGE_SIZE)
        cache_vt = cache_v[l].transpose(0, 2, 3, 1).reshape(n_phys, ATT_WIDTH, PAGE_SIZE)
        att = _moba_sample(q.reshape(db, DEC_SEQ, ATT_WIDTH), k.reshape(db, DEC_SEQ, ATT_WIDTH),
                           v.reshape(db, DEC_SEQ, ATT_WIDTH), cache_kt, cache_vt, page_table)
        hs = _mix_out(pool, att.reshape(db * DEC_SEQ, ATT_WIDTH), w_out_bf, vec(g_post_mix), hs)
        mem_native = lambda a: a[l].reshape(db, MEM_LEN, MEM_HEADS, MEM_CHUNKS, LANES).transpose(
            0, 1, 3, 2, 4).reshape(db, MEM_LEN, MEM_ROWS, LANES)
        hs = _xattn_sample(hs, vec(g_pre_x), w_xq_bf, mem_native(cache_mem_k), mem_native(cache_mem_v),
                           w_xo_bf, vec(g_post_x), n_mem=4)
        hs = _ffn(hs, vec(g_pre_ffn), w_up_bf, w_down_bf, vec(g_post_ffn))
        pool_s.append(u_ext[:, u_ext.shape[1] - POOL_STATE:])
        k_s.append(k.reshape(db, DEC_SEQ, N_HEADS, HEAD_DIM))
        v_s.append(v.reshape(db, DEC_SEQ, N_HEADS, HEAD_DIM))

    return (hp.reshape(batch, SEQ, D_MODEL), hs.reshape(db, DEC_SEQ, D_MODEL),
            jnp.stack(pool_p), jnp.stack(k_p), jnp.stack(v_p), jnp.stack(mk_p), jnp.stack(mv_p),
            jnp.stack(pool_s), jnp.stack(k_s), jnp.stack(v_s))
```

```python
import functools

import numpy as np
import jax
import jax.numpy as jnp
from jax import lax
from jax.experimental import pallas as pl
from jax.experimental.pallas import tpu as pltpu

F32 = jnp.float32
BF16 = jnp.bfloat16

D_MODEL = 1024
SEQ = 8192
DEC_SEQ = 8
PAST_LEN = 8192
PAGE_SIZE = 128
POOL_WIDTH = 512
POOL_WINDOWS = (2, 4, 8, 16)
POOL_GW = 128
POOL_STATE = 15
ATT_WIDTH = 512
N_HEADS = 8
HEAD_DIM = 64
ROT_DIM = 16
ROPE_THETA = 500000.0
MOBA_BLOCK = 256
MOBA_TOPK = 3
MEM_LEN = 256
MEM_HEADS = 4
MEM_HEAD_DIM = 256
D_FF = 4096
EPS = 1e-6

LANES = 128
SUBLANES = 8

ROW_TILE = 512
POOL_HALO = 16
N_PAST_BLOCKS = PAST_LEN // MOBA_BLOCK
PAGES_PER_BLOCK = MOBA_BLOCK // PAGE_SIZE
GATE_PAD = LANES
KV_SLOTS = 32
KV_UNROLL = 4
PAST_GROUP = 8
PAST_STEP = 2
PROMPT_HEADS = 4
ONES_ROWS = 16
ROT_HALF = ROT_DIM // 2
MEM_CHUNKS = MEM_HEAD_DIM // LANES
MEM_ROWS = MEM_CHUNKS * MEM_HEADS
NEG_INF = float("-inf")
MASK_BIAS = -1e30
LOG2_E = 1.4426950408889634

assert PAST_LEN % MOBA_BLOCK == 0, "sample own-block is assumed to hold new keys only"
assert PAST_LEN + 1 >= max(POOL_WINDOWS)
assert ROT_HALF == SUBLANES, "feature-major rotary assumes one sublane group per rotary half"
assert (2 * N_PAST_BLOCKS) % KV_SLOTS == 0 and N_PAST_BLOCKS % KV_UNROLL == 0 and KV_SLOTS % KV_UNROLL == 0
assert (SEQ // MOBA_BLOCK) % PAST_GROUP == 0 and HEAD_DIM + SEQ // MOBA_BLOCK <= LANES


def _params(n_grid, vmem_mib):
    return pltpu.CompilerParams(
        dimension_semantics=("arbitrary",) * n_grid,
        vmem_limit_bytes=vmem_mib * 1024 * 1024,
    )


def _rmsnorm(x, g):
    ms = jnp.mean(x * x, axis=-1, keepdims=True)
    return x * lax.rsqrt(ms + EPS) * g


def _dot(a, b):
    return jnp.dot(a, b, preferred_element_type=F32)


def _dot_nt(a, b, precision=None):
    return lax.dot_general(a, b, (((1,), (1,)), ((), ())), precision=precision,
                           preferred_element_type=F32)


def _topk_mask(gate, valid, axis):
    n = gate.shape[axis]
    idx = lax.broadcasted_iota(jnp.int32, gate.shape, axis).astype(F32)
    g = jnp.where(valid, gate, NEG_INF)
    sel = jnp.zeros(gate.shape, F32)
    for _ in range(MOBA_TOPK):
        mx = jnp.max(g, axis=axis, keepdims=True)
        first = jnp.min(jnp.where(g == mx, idx, float(n)), axis=axis, keepdims=True)
        pick = idx == first
        sel = jnp.where(pick, 1.0, sel)
        g = jnp.where(pick, NEG_INF, g)
    return jnp.where(valid, sel, 0.0)


def _rope_rows(z, pos_rows, inv_lanes):
    tm = z.shape[0]
    ang = pos_rows * inv_lanes
    cos = jnp.cos(ang)
    sin = jnp.sin(ang)
    d = lax.rem(lax.broadcasted_iota(jnp.int32, (tm, LANES), 1), HEAD_DIM)
    sin_lo = jnp.where(d < ROT_HALF, -sin, 0.0)
    sin_hi = jnp.where((d >= ROT_HALF) & (d < ROT_DIM), sin, 0.0)
    out = []
    for c in range(ATT_WIDTH // LANES):
        zz = z[:, c * LANES:(c + 1) * LANES]
        out.append(zz * cos
                   + pltpu.roll(zz, LANES - ROT_HALF, 1) * sin_lo
                   + pltpu.roll(zz, ROT_HALF, 1) * sin_hi)
    return out


def _in_proj_rows_kernel(x_ref, g_ref, w_ref, inv_ref, u_ref, q_ref, k_ref, v_ref, *, period, offset):
    tm = x_ref.shape[0]
    i = pl.program_id(0)
    n = _rmsnorm(x_ref[...], g_ref[...]).astype(BF16)

    def seg(j):
        return _dot(n, w_ref[:, j * ATT_WIDTH:(j + 1) * ATT_WIDTH])

    u_ref[...] = seg(0)
    v_ref[...] = seg(3)
    row = lax.broadcasted_iota(jnp.int32, (tm, LANES), 0) + i * tm
    pos = (lax.rem(row, period) + offset).astype(F32)
    for j, ref in ((1, q_ref), (2, k_ref)):
        for c, piece in enumerate(_rope_rows(seg(j), pos, inv_ref[...])):
            ref[:, c * LANES:(c + 1) * LANES] = piece


def _in_proj_rows(x, g, w_bf, inv_lanes, period, offset):
    m = x.shape[0]
    tm = min(ROW_TILE, m)
    out = jax.ShapeDtypeStruct((m, ATT_WIDTH), F32)
    row_spec = pl.BlockSpec((tm, ATT_WIDTH), lambda i: (i, 0))
    return pl.pallas_call(
        functools.partial(_in_proj_rows_kernel, period=period, offset=offset),
        grid=(m // tm,),
        in_specs=[
            pl.BlockSpec((tm, D_MODEL), lambda i: (i, 0)),
            pl.BlockSpec((1, D_MODEL), lambda i: (0, 0)),
            pl.BlockSpec((D_MODEL, 4 * ATT_WIDTH), lambda i: (0, 0)),
            pl.BlockSpec((1, LANES), lambda i: (0, 0)),
        ],
        out_specs=[row_spec] * 4,
        out_shape=[out] * 4,
        compiler_params=_params(1, 40),
        name="in_proj_rows",
    )(x, g, w_bf, inv_lanes)


def _in_proj_prompt_kernel(x_ref, g_ref, wu_ref, wqkvt_ref, invc_ref, wpool_ref, sc_ref,
                           pool_ref, tail_ref, qt_ref, kt_ref, vt_ref, ext_ref):
    tm = x_ref.shape[0]
    i = pl.program_id(1)
    n = _rmsnorm(x_ref[...], g_ref[...]).astype(BF16)
    u = _dot(n, wu_ref[...])

    @pl.when(i == 0)
    def _():
        ext_ref[0:POOL_HALO, :] = jnp.zeros((POOL_HALO, POOL_WIDTH), F32)

    @pl.when(i > 0)
    def _():
        ext_ref[0:POOL_HALO, :] = ext_ref[tm:tm + POOL_HALO, :]

    ext_ref[POOL_HALO:, :] = u
    tail_ref[0] = u[tm - POOL_HALO:, :]
    _pool_mix_rows(u, ext_ref, i * tm, wpool_ref, sc_ref, pool_ref)

    vt_ref[0] = _dot_nt(wqkvt_ref[2 * ATT_WIDTH:, :], n)
    pos_t = (lax.broadcasted_iota(jnp.int32, (ROT_HALF, tm), 1) + i * tm).astype(F32)
    ang = invc_ref[...] * pos_t
    cos = jnp.cos(ang)
    sin = jnp.sin(ang)
    for off, ref in ((0, qt_ref), (ATT_WIDTH, kt_ref)):
        zt = _dot_nt(wqkvt_ref[off:off + ATT_WIDTH, :], n)
        for h in range(N_HEADS):
            r0 = h * HEAD_DIM
            x1 = zt[r0:r0 + ROT_HALF, :]
            x2 = zt[r0 + ROT_HALF:r0 + ROT_DIM, :]
            ref[0, r0:r0 + ROT_HALF, :] = x1 * cos - x2 * sin
            ref[0, r0 + ROT_HALF:r0 + ROT_DIM, :] = x2 * cos + x1 * sin
            ref[0, r0 + ROT_DIM:r0 + HEAD_DIM, :] = zt[r0 + ROT_DIM:r0 + HEAD_DIM, :]


def _in_proj_prompt(x, g, wu_bf, wqkvt_bf, inv_col, w_pool_bf, pool_scale, batch):
    tm = ROW_TILE
    tiles = SEQ // tm
    t_spec = pl.BlockSpec((1, ATT_WIDTH, tm), lambda b, i: (b, 0, i))
    feat = jax.ShapeDtypeStruct((batch, ATT_WIDTH, SEQ), F32)
    return pl.pallas_call(
        _in_proj_prompt_kernel,
        grid=(batch, tiles),
        in_specs=[
            pl.BlockSpec((tm, D_MODEL), lambda b, i: (b * tiles + i, 0)),
            pl.BlockSpec((1, D_MODEL), lambda b, i: (0, 0)),
            pl.BlockSpec((D_MODEL, POOL_WIDTH), lambda b, i: (0, 0)),
            pl.BlockSpec((3 * ATT_WIDTH, D_MODEL), lambda b, i: (0, 0)),
            pl.BlockSpec((ROT_HALF, 1), lambda b, i: (0, 0)),
            pl.BlockSpec((len(POOL_WINDOWS), POOL_GW, POOL_GW), lambda b, i: (0, 0, 0)),
            pl.BlockSpec((1, POOL_WIDTH), lambda b, i: (0, 0)),
        ],
        out_specs=[pl.BlockSpec((tm, POOL_WIDTH), lambda b, i: (b * tiles + i, 0)),
                   pl.BlockSpec((1, POOL_HALO, POOL_WIDTH), lambda b, i: (b, 0, 0)),
                   t_spec, t_spec, t_spec],
        out_shape=[jax.ShapeDtypeStruct((batch * SEQ, POOL_WIDTH), F32),
                   jax.ShapeDtypeStruct((batch, POOL_HALO, POOL_WIDTH), F32), feat, feat, feat],
        scratch_shapes=[pltpu.VMEM((POOL_HALO + tm, POOL_WIDTH), F32)],
        compiler_params=_params(2, 40),
        name="in_proj_prompt",
    )(x, g, wu_bf, wqkvt_bf, inv_col, w_pool_bf, pool_scale)


def _pool_mix_rows(u, ext_ref, pos0, w_ref, sc_ref, o_ref):
    tm = u.shape[0]
    pos = pos0 + lax.broadcasted_iota(jnp.int32, (tm, POOL_GW), 0)
    for g, w in enumerate(POOL_WINDOWS):
        cols = slice(g * POOL_GW, (g + 1) * POOL_GW)
        cur = u[:, cols]
        s = cur
        for k in range(1, w):
            s = s + ext_ref[POOL_HALO - k:POOL_HALO - k + tm, cols]
        cnt = jnp.minimum(pos + 1, w).astype(F32)
        d = s / cnt - cur
        o_ref[:, cols] = _dot(d.astype(BF16), w_ref[g]) * sc_ref[:, cols]


def _pool_sample_kernel(e_ref, w_ref, sc_ref, o_ref):
    nb = e_ref.shape[0]
    lo = 1 + POOL_STATE
    t = lax.broadcasted_iota(jnp.int32, (nb, DEC_SEQ, POOL_GW), 1)
    for g, w in enumerate(POOL_WINDOWS):
        cols = slice(g * POOL_GW, (g + 1) * POOL_GW)
        cur = e_ref[:, lo:lo + DEC_SEQ, cols]
        s = cur
        for k in range(1, w):
            s = s + e_ref[:, lo - k:lo - k + DEC_SEQ, cols]
        cnt = jnp.minimum(PAST_LEN + t + 1, w).astype(F32)
        d = (s / cnt - cur).reshape(nb * DEC_SEQ, POOL_GW)
        o_ref[:, cols] = _dot(d.astype(BF16), w_ref[g]) * sc_ref[:, cols]


def _pool_sample(ext, w_pool_bf, pool_scale):
    db, rows, _ = ext.shape
    nb = 32
    return pl.pallas_call(
        _pool_sample_kernel,
        grid=(db // nb,),
        in_specs=[
            pl.BlockSpec((nb, rows, POOL_WIDTH), lambda i: (i, 0, 0)),
            pl.BlockSpec((len(POOL_WINDOWS), POOL_GW, POOL_GW), lambda i: (0, 0, 0)),
            pl.BlockSpec((1, POOL_WIDTH), lambda i: (0, 0)),
        ],
        out_specs=pl.BlockSpec((nb * DEC_SEQ, POOL_WIDTH), lambda i: (i, 0)),
        out_shape=jax.ShapeDtypeStruct((db * DEC_SEQ, POOL_WIDTH), F32),
        compiler_params=_params(1, 32),
        name="pool_sample",
    )(ext, w_pool_bf, pool_scale)


def _moba_prompt_kernel(qt_ref, kt_ref, vt_ref, o_ref, kaug_ref, kmean_ref):
    tq = MOBA_BLOCK
    blk = pl.program_id(2)
    n_blocks = SEQ // MOBA_BLOCK
    pad_rows = LANES - HEAD_DIM - n_blocks

    heads = range(PROMPT_HEADS)
    head_rows = [slice(a * HEAD_DIM, (a + 1) * HEAD_DIM) for a in heads]

    @pl.when(blk == 0)
    def _():
        block_row = lax.broadcasted_iota(jnp.int32, (n_blocks, MOBA_BLOCK), 0)
        lane = lax.broadcasted_iota(jnp.int32, (1, LANES), 1)
        for a in heads:
            for n in range(n_blocks):
                kb = kt_ref[0, head_rows[a], n * MOBA_BLOCK:(n + 1) * MOBA_BLOCK]
                aug = jnp.concatenate([kb, jnp.where(block_row == n, 1.0, 0.0),
                                       jnp.zeros((pad_rows, MOBA_BLOCK), F32)], axis=0).T
                kaug_ref[a, n * MOBA_BLOCK:(n + 1) * MOBA_BLOCK, :] = aug.astype(BF16)
                kmean_ref[a, n:n + 1, :] = jnp.where(
                    lane < HEAD_DIM, jnp.sum(aug, axis=0, keepdims=True) * (1.0 / MOBA_BLOCK), 0.0)

    n_idx = lax.broadcasted_iota(jnp.int32, (n_blocks, tq), 0)
    w_past, w_own = [], []
    for a in heads:
        qt = qt_ref[0, head_rows[a], :]
        gate = jnp.dot(kmean_ref[a], jnp.concatenate([qt, jnp.zeros((LANES - HEAD_DIM, tq), F32)], axis=0),
                       precision=lax.Precision.HIGHEST, preferred_element_type=F32)
        sel = _topk_mask(gate, n_idx < blk, axis=0)
        bias = jnp.where(sel > 0.5, 0.0, MASK_BIAS)
        q_log2 = qt * (HEAD_DIM ** -0.5 * LOG2_E)
        w_past.append(jnp.concatenate([q_log2, bias, jnp.zeros((pad_rows, tq), F32)], axis=0).astype(BF16))
        w_own.append(jnp.concatenate([q_log2, jnp.zeros((LANES - HEAD_DIM, tq), F32)], axis=0).astype(BF16))

    def scores(a, w, start, size):
        return _dot(kaug_ref[a, pl.ds(pl.multiple_of(start, MOBA_BLOCK), size), :], w)

    def values(a, start, size):
        vb = vt_ref[0, head_rows[a], pl.ds(pl.multiple_of(start, MOBA_BLOCK), size)]
        return jnp.concatenate([vb.astype(BF16), jnp.ones((ONES_ROWS, size), BF16)], axis=0)

    key_pos = lax.broadcasted_iota(jnp.int32, (MOBA_BLOCK, tq), 0)
    q_pos = lax.broadcasted_iota(jnp.int32, (MOBA_BLOCK, tq), 1)
    own = [jnp.where(key_pos <= q_pos, scores(a, w_own[a], blk * MOBA_BLOCK, MOBA_BLOCK), NEG_INF) for a in heads]
    init = []
    for a in heads:
        m0 = jnp.max(own[a], axis=0, keepdims=True)
        init += [m0, _dot(values(a, blk * MOBA_BLOCK, MOBA_BLOCK), jnp.exp2(own[a] - m0).astype(BF16))]

    span = PAST_GROUP * MOBA_BLOCK
    step = PAST_STEP * MOBA_BLOCK

    def body(i, carry):
        carry = list(carry)
        starts = [i * span + u * step for u in range(PAST_GROUP // PAST_STEP)]
        tiles = [[scores(a, w_past[a], start, step) for start in starts] for a in heads]
        for u, start in enumerate(starts):
            for a in heads:
                m, acc = carry[2 * a:2 * a + 2]
                s = tiles[a][u]
                m_new = jnp.maximum(m, jnp.max(s, axis=0, keepdims=True))
                p = jnp.exp2(s - m_new).astype(BF16)
                carry[2 * a:2 * a + 2] = [m_new, jnp.exp2(m - m_new) * acc + _dot(values(a, start, step), p)]
        return tuple(carry)

    res = lax.fori_loop(0, (blk + PAST_GROUP - 1) // PAST_GROUP, body, tuple(init))
    for a in heads:
        acc = res[2 * a + 1]
        o_ref[0, head_rows[a], :] = acc[0:HEAD_DIM, :] / acc[HEAD_DIM:HEAD_DIM + 1, :]


def _moba_prompt(qt, kt, vt):
    b = qt.shape[0]
    rows = PROMPT_HEADS * HEAD_DIM
    kv_spec = pl.BlockSpec((1, rows, SEQ), lambda bi, h, qi: (bi, h, 0))
    q_spec = pl.BlockSpec((1, rows, MOBA_BLOCK), lambda bi, h, qi: (bi, h, qi))
    return pl.pallas_call(
        _moba_prompt_kernel,
        grid=(b, N_HEADS // PROMPT_HEADS, SEQ // MOBA_BLOCK),
        in_specs=[q_spec, kv_spec, kv_spec],
        out_specs=q_spec,
        out_shape=jax.ShapeDtypeStruct((b, ATT_WIDTH, SEQ), F32),
        scratch_shapes=[pltpu.VMEM((PROMPT_HEADS, SEQ, LANES), BF16),
                        pltpu.VMEM((PROMPT_HEADS, SEQ // MOBA_BLOCK, LANES), F32)],
        compiler_params=_params(3, 58),
        name="moba_prompt",
    )(qt, kt, vt)


def _moba_sample_kernel(pt_ref, q_ref, kn_ref, vn_ref, ck_hbm, cv_hbm, o_ref,
                        buf, sem, s_ref, p_ref):
    b = pl.program_id(0)
    n_seq = pl.num_programs(0)
    nb = N_PAST_BLOCKS
    n_rows = N_HEADS * DEC_SEQ
    scale = HEAD_DIM ** -0.5

    def page_copy(cache, seq, g, half):
        slot = lax.rem(g, KV_SLOTS)
        page = pt_ref[seq, lax.rem(g, nb) * PAGES_PER_BLOCK + half]
        return pltpu.make_async_copy(
            cache.at[page],
            buf.at[slot, :, pl.ds(half * PAGE_SIZE, PAGE_SIZE)],
            sem.at[slot, half])

    def start_chunk(seq, g):
        @pl.when(g < nb)
        def _():
            for half in range(PAGES_PER_BLOCK):
                page_copy(ck_hbm, seq, g, half).start()

        @pl.when(g >= nb)
        def _():
            for half in range(PAGES_PER_BLOCK):
                page_copy(cv_hbm, seq, g, half).start()

    def start_ahead(g):
        nxt = g + KV_SLOTS

        @pl.when(nxt < 2 * nb)
        def _():
            start_chunk(b, nxt)

        @pl.when((nxt >= 2 * nb) & (b + 1 < n_seq))
        def _():
            start_chunk(b + 1, nxt - 2 * nb)

    def wait_chunk(g):
        for half in range(PAGES_PER_BLOCK):
            page_copy(ck_hbm, b, g, half).wait()

    @pl.when(b == 0)
    def _():
        for g in range(KV_SLOTS):
            start_chunk(b, jnp.int32(g))

    q = q_ref[0]
    lane_head = lax.broadcasted_iota(jnp.int32, (DEC_SEQ, ATT_WIDTH), 1) // HEAD_DIM
    qbd = jnp.concatenate([jnp.where(lane_head == h, q, 0.0) for h in range(N_HEADS)], axis=0)
    q_hi = qbd.astype(BF16)
    q_lo = (qbd - q_hi.astype(F32)).astype(BF16)
    q_stack = jnp.concatenate([q_hi, q_lo], axis=0)

    def raw_scores(stacked):
        return stacked[0:n_rows] + stacked[n_rows:]

    gate_idx = lax.broadcasted_iota(jnp.int32, (n_rows, GATE_PAD), 1)

    def k_body(i, gate):
        chunks = [i * KV_UNROLL + u for u in range(KV_UNROLL)]
        for g in chunks:
            wait_chunk(g)
        for g in chunks:
            kb = buf[lax.rem(g, KV_SLOTS)].astype(BF16)
            s = raw_scores(_dot(q_stack, kb))
            s_ref[g] = s * scale
            gate = jnp.where(gate_idx == g, jnp.sum(s, axis=1, keepdims=True) * (1.0 / MOBA_BLOCK), gate)
        for g in chunks:
            start_ahead(g)
        return gate

    gate = lax.fori_loop(0, nb // KV_UNROLL, k_body, jnp.zeros((n_rows, GATE_PAD), F32))
    sel = _topk_mask(gate, gate_idx < nb, axis=1)

    pad = jnp.zeros((LANES - DEC_SEQ, ATT_WIDTH), F32)
    kn = jnp.concatenate([kn_ref[0], pad], axis=0).astype(BF16)
    vn = jnp.concatenate([vn_ref[0], pad], axis=0).astype(BF16)
    t_row = lax.rem(lax.broadcasted_iota(jnp.int32, (n_rows, LANES), 0), DEC_SEQ)
    t_col = lax.broadcasted_iota(jnp.int32, (n_rows, LANES), 1)
    s_own = jnp.where(t_col <= t_row, raw_scores(_dot_nt(q_stack, kn)) * scale, NEG_INF)

    m_lanes = jnp.full((n_rows, MOBA_BLOCK), NEG_INF, F32)
    for n in range(nb):
        m_lanes = jnp.maximum(m_lanes, jnp.where(sel[:, n:n + 1] > 0.5, s_ref[n], NEG_INF))
    m = jnp.maximum(jnp.max(s_own, axis=1, keepdims=True), jnp.max(m_lanes, axis=1, keepdims=True))
    p_own = jnp.exp(s_own - m)
    l_lanes = jnp.zeros((n_rows, MOBA_BLOCK), F32)
    for n in range(nb):
        pn = jnp.exp(jnp.where(sel[:, n:n + 1] > 0.5, s_ref[n], NEG_INF) - m)
        l_lanes = l_lanes + pn
        p_ref[n] = pn.astype(BF16)
    l = jnp.sum(p_own, axis=1, keepdims=True) + jnp.sum(l_lanes, axis=1, keepdims=True)

    def v_body(i, acc):
        chunks = [nb + i * KV_UNROLL + u for u in range(KV_UNROLL)]
        for g in chunks:
            wait_chunk(g)
        for g in chunks:
            vb = buf[lax.rem(g, KV_SLOTS)].astype(BF16)
            acc = acc + _dot_nt(p_ref[g - nb], vb)
        for g in chunks:
            start_ahead(g)
        return acc

    acc = lax.fori_loop(0, nb // KV_UNROLL, v_body, _dot(p_own.astype(BF16), vn))
    o = acc / l
    out = jnp.zeros((DEC_SEQ, ATT_WIDTH), F32)
    for h in range(N_HEADS):
        out = jnp.where(lane_head == h, o[h * DEC_SEQ:(h + 1) * DEC_SEQ, :], out)
    o_ref[0] = out


def _moba_sample(q, k_new, v_new, cache_kt, cache_vt, page_table):
    db = q.shape[0]
    n_rows = N_HEADS * DEC_SEQ
    tok_spec = pl.BlockSpec((1, DEC_SEQ, ATT_WIDTH), lambda b, pt: (b, 0, 0))
    grid_spec = pltpu.PrefetchScalarGridSpec(
        num_scalar_prefetch=1,
        grid=(db,),
        in_specs=[tok_spec, tok_spec, tok_spec,
                  pl.BlockSpec(memory_space=pl.ANY), pl.BlockSpec(memory_space=pl.ANY)],
        out_specs=tok_spec,
        scratch_shapes=[
            pltpu.VMEM((KV_SLOTS, ATT_WIDTH, MOBA_BLOCK), F32),
            pltpu.SemaphoreType.DMA((KV_SLOTS, PAGES_PER_BLOCK)),
            pltpu.VMEM((N_PAST_BLOCKS, n_rows, MOBA_BLOCK), F32),
            pltpu.VMEM((N_PAST_BLOCKS, n_rows, MOBA_BLOCK), BF16),
        ],
    )
    return pl.pallas_call(
        _moba_sample_kernel,
        grid_spec=grid_spec,
        out_shape=jax.ShapeDtypeStruct((db, DEC_SEQ, ATT_WIDTH), F32),
        compiler_params=_params(1, 40),
        name="moba_sample",
    )(page_table, q, k_new, v_new, cache_kt, cache_vt)


def _mix_out_rows(pool, att, w_ref, g, res):
    y = _dot(pool.astype(BF16), w_ref[0:POOL_WIDTH, :])
    y = y + _dot(att.astype(BF16), w_ref[POOL_WIDTH:, :])
    return res + _rmsnorm(y, g)


def _mix_out_kernel(a_ref, b_ref, w_ref, g_ref, r_ref, o_ref):
    o_ref[...] = _mix_out_rows(a_ref[...], b_ref[...], w_ref, g_ref[...], r_ref[...])


def _mix_out(pool, att, w_bf, g, res):
    m = res.shape[0]
    tm = min(ROW_TILE, m)
    half_spec = pl.BlockSpec((tm, POOL_WIDTH), lambda i: (i, 0))
    row_spec = pl.BlockSpec((tm, D_MODEL), lambda i: (i, 0))
    return pl.pallas_call(
        _mix_out_kernel,
        grid=(m // tm,),
        in_specs=[half_spec, half_spec,
                  pl.BlockSpec((D_MODEL, D_MODEL), lambda i: (0, 0)),
                  pl.BlockSpec((1, D_MODEL), lambda i: (0, 0)),
                  row_spec],
        out_specs=row_spec,
        out_shape=jax.ShapeDtypeStruct((m, D_MODEL), F32),
        compiler_params=_params(1, 32),
        name="mix_out",
    )(pool, att, w_bf, g, res)


def _mem_kv_kernel(x_ref, g_ref, wk_ref, wv_ref, k_ref, v_ref):
    n = _rmsnorm(x_ref[...], g_ref[...]).astype(BF16)
    k_ref[...] = _dot(n, wk_ref[...])
    v_ref[...] = _dot(n, wv_ref[...])


def _mem_kv(mem, g, wk_bf, wv_bf):
    m = mem.shape[0]
    tm = MEM_LEN
    row_spec = pl.BlockSpec((tm, D_MODEL), lambda i: (i, 0))
    w_spec = pl.BlockSpec((D_MODEL, D_MODEL), lambda i: (0, 0))
    out = jax.ShapeDtypeStruct((m, D_MODEL), F32)
    return pl.pallas_call(
        _mem_kv_kernel,
        grid=(m // tm,),
        in_specs=[row_spec, pl.BlockSpec((1, D_MODEL), lambda i: (0, 0)), w_spec, w_spec],
        out_specs=[row_spec, row_spec],
        out_shape=[out, out],
        compiler_params=_params(1, 32),
        name="mem_kv",
    )(mem, g, wk_bf, wv_bf)


def _xattn_rows(h, gpre, wq_ref, mk_ref, mv_ref, wo_ref, gpost):
    scale = MEM_HEAD_DIM ** -0.5
    q = _dot(_rmsnorm(h, gpre).astype(BF16), wq_ref[...]).astype(BF16)
    y = jnp.zeros(h.shape, F32)
    for hh in range(MEM_HEADS):
        cols = slice(hh * MEM_HEAD_DIM, (hh + 1) * MEM_HEAD_DIM)
        s = _dot_nt(q[:, cols], mk_ref[0, :, cols].astype(BF16)) * scale
        p = jnp.exp(s - jnp.max(s, axis=-1, keepdims=True))
        p = (p / jnp.sum(p, axis=-1, keepdims=True)).astype(BF16)
        oh = _dot(p, mv_ref[0, :, cols].astype(BF16))
        y = y + _dot(oh.astype(BF16), wo_ref[cols, :])
    return h + _rmsnorm(y, gpost)

def _xattn_sample_kernel(h_ref, gpre_ref, wq_ref, mk_ref, mv_ref, wo_ref, gpost_ref, o_ref):
    tm = h_ref.shape[0]
    n_mem = mk_ref.shape[0]
    t_rows = tm // n_mem
    width = MEM_LEN * MEM_ROWS
    n_q = MEM_HEADS * t_rows
    scale = MEM_HEAD_DIM ** -0.5
    h = h_ref[...]
    q = _dot(_rmsnorm(h, gpre_ref[...]).astype(BF16), wq_ref[...]).astype(BF16)
    col = lax.broadcasted_iota(jnp.int32, (n_q, width), 1)
    row_head = lax.broadcasted_iota(jnp.int32, (n_q, width), 0) // t_rows
    own = (lax.rem(col, MEM_ROWS) == row_head)
    rows = []
    for b in range(n_mem):
        keys = mk_ref[b].reshape(width, LANES).astype(BF16)
        vals = mv_ref[b].reshape(width, LANES).astype(BF16)
        qb = q[b * t_rows:(b + 1) * t_rows, :]
        s = None
        for c in range(MEM_CHUNKS):
            qc = jnp.concatenate([qb[:, hh * MEM_HEAD_DIM + c * LANES:hh * MEM_HEAD_DIM + (c + 1) * LANES]
                                  for hh in range(MEM_HEADS)], axis=0)
            sc = _dot_nt(qc, keys)
            if c:
                sc = pltpu.roll(sc, width - c * MEM_HEADS, 1)
            s = sc if s is None else s + sc
        s = jnp.where(own, s * scale, NEG_INF)
        p = jnp.exp(s - jnp.max(s, axis=-1, keepdims=True))
        p = p / jnp.sum(p, axis=-1, keepdims=True)
        pieces = [[None] * MEM_CHUNKS for _ in range(MEM_HEADS)]
        for c in range(MEM_CHUNKS):
            pc = pltpu.roll(p, c * MEM_HEADS, 1) if c else p
            oc = _dot(pc.astype(BF16), vals)
            for hh in range(MEM_HEADS):
                pieces[hh][c] = oc[hh * t_rows:(hh + 1) * t_rows, :]
        rows.append(jnp.concatenate([pc for head in pieces for pc in head], axis=1))
    o = jnp.concatenate(rows, axis=0)
    o_ref[...] = h + _rmsnorm(_dot(o.astype(BF16), wo_ref[...]), gpost_ref[...])


def _xattn_sample(h, gpre, wq_bf, mk, mv, wo_bf, gpost, n_mem):
    m = h.shape[0]
    tm = n_mem * (m // mk.shape[0])
    row_spec = pl.BlockSpec((tm, D_MODEL), lambda i: (i, 0))
    vec_spec = pl.BlockSpec((1, D_MODEL), lambda i: (0, 0))
    w_spec = pl.BlockSpec((D_MODEL, D_MODEL), lambda i: (0, 0))
    mem_spec = pl.BlockSpec((n_mem, MEM_LEN, MEM_ROWS, LANES), lambda i: (i, 0, 0, 0))
    return pl.pallas_call(
        _xattn_sample_kernel,
        grid=(m // tm,),
        in_specs=[row_spec, vec_spec, w_spec, mem_spec, mem_spec, w_spec, vec_spec],
        out_specs=row_spec,
        out_shape=jax.ShapeDtypeStruct((m, D_MODEL), F32),
        compiler_params=_params(1, 48),
        name="xattn_sample",
    )(h, gpre, wq_bf, mk, mv, wo_bf, gpost)


def _ffn_rows(h, gpre, wu_ref, wd_ref, gpost):
    n = _rmsnorm(h, gpre).astype(BF16)
    y = jnp.zeros(h.shape, F32)
    chunk = D_MODEL
    for c in range(D_FF // chunk):
        a = jnp.square(jnp.maximum(_dot(n, wu_ref[:, c * chunk:(c + 1) * chunk]), 0.0))
        y = y + _dot(a.astype(BF16), wd_ref[c * chunk:(c + 1) * chunk, :])
    return h + _rmsnorm(y, gpost)


def _ffn_kernel(h_ref, gpre_ref, wu_ref, wd_ref, gpost_ref, o_ref):
    o_ref[...] = _ffn_rows(h_ref[...], gpre_ref[...], wu_ref, wd_ref, gpost_ref[...])


def _post_prompt_kernel(pool_ref, att_ref, x_ref, wout_ref, gmix_ref, gprex_ref, wq_ref, mk_ref, mv_ref, wo_ref,
                        gpostx_ref, gpreffn_ref, wu_ref, wd_ref, gpostffn_ref, o_ref):
    h = _mix_out_rows(pool_ref[...], att_ref[0].T, wout_ref, gmix_ref[...], x_ref[...])
    h = _xattn_rows(h, gprex_ref[...], wq_ref, mk_ref, mv_ref, wo_ref, gpostx_ref[...])
    o_ref[...] = _ffn_rows(h, gpreffn_ref[...], wu_ref, wd_ref, gpostffn_ref[...])


def _post_prompt(pool, att_t, x, w_out_bf, g_mix, g_pre_x, wq_bf, mk, mv, wo_bf, g_post_x,
                 g_pre_ffn, wu_bf, wd_bf, g_post_ffn):
    m = x.shape[0]
    tm = ROW_TILE
    tiles = att_t.shape[2] // tm

    def resident(shape):
        return pl.BlockSpec(shape, lambda i: (0,) * len(shape), pipeline_mode=pl.Buffered(1))

    row_spec = pl.BlockSpec((tm, D_MODEL), lambda i: (i, 0))
    vec_spec = resident((1, D_MODEL))
    sq_spec = resident((D_MODEL, D_MODEL))
    mem_spec = pl.BlockSpec((1, MEM_LEN, D_MODEL), lambda i: (i // tiles, 0, 0))
    return pl.pallas_call(
        _post_prompt_kernel,
        grid=(m // tm,),
        in_specs=[pl.BlockSpec((tm, POOL_WIDTH), lambda i: (i, 0)),
                  pl.BlockSpec((1, ATT_WIDTH, tm), lambda i: (i // tiles, 0, i % tiles)),
                  row_spec, sq_spec, vec_spec, vec_spec, sq_spec, mem_spec, mem_spec, sq_spec, vec_spec,
                  vec_spec, resident((D_MODEL, D_FF)), resident((D_FF, D_MODEL)), vec_spec],
        out_specs=row_spec,
        out_shape=jax.ShapeDtypeStruct((m, D_MODEL), F32),
        compiler_params=_params(1, 58),
        name="post_prompt",
    )(pool, att_t, x, w_out_bf, g_mix, g_pre_x, wq_bf, mk, mv, wo_bf, g_post_x, g_pre_ffn, wu_bf, wd_bf, g_post_ffn)


def _ffn(h, gpre, wu_bf, wd_bf, gpost):
    m = h.shape[0]
    tm = min(ROW_TILE, m)
    row_spec = pl.BlockSpec((tm, D_MODEL), lambda i: (i, 0))
    vec_spec = pl.BlockSpec((1, D_MODEL), lambda i: (0, 0))
    return pl.pallas_call(
        _ffn_kernel,
        grid=(m // tm,),
        in_specs=[row_spec, vec_spec,
                  pl.BlockSpec((D_MODEL, D_FF), lambda i: (0, 0)),
                  pl.BlockSpec((D_FF, D_MODEL), lambda i: (0, 0)),
                  vec_spec],
        out_specs=row_spec,
        out_shape=jax.ShapeDtypeStruct((m, D_MODEL), F32),
        compiler_params=_params(1, 56),
        name="ffn",
    )(h, gpre, wu_bf, wd_bf, gpost)


def _rope_inv():
    inv = ROPE_THETA ** (-2.0 * jnp.arange(ROT_HALF, dtype=F32) / ROT_DIM)
    d = np.arange(LANES) % HEAD_DIM
    rotary = (d < ROT_DIM).astype(np.float32)
    inv_lanes = (inv[d % ROT_HALF] * rotary).reshape(1, LANES)
    return inv_lanes, inv.reshape(ROT_HALF, 1)


def kernel(x_prompt, x_sample, mem_prompt, cache_k, cache_v, cache_mem_k, cache_mem_v, state_pool, page_table,
           g_pre_mix, w_in, w_pool, pool_scale, w_out, g_post_mix,
           g_mem, g_pre_x, w_xq, w_xk, w_xv, w_xo, g_post_x,
           g_pre_ffn, w_up, w_down, g_post_ffn):
    depth = w_in.shape[0]
    batch = x_prompt.shape[0]
    db = x_sample.shape[0]
    n_phys = cache_k.shape[1]
    inv_lanes, inv_col = _rope_inv()

    hp = x_prompt.reshape(batch * SEQ, D_MODEL)
    hs = x_sample.reshape(db * DEC_SEQ, D_MODEL)
    pool_p, k_p, v_p, mk_p, mv_p, pool_s, k_s, v_s = ([] for _ in range(8))
    for l in range(depth):
        vec = lambda a: a[l].reshape(1, -1)
        w_in_bf = w_in[l].astype(BF16)
        w_u_bf = w_in_bf[:, :POOL_WIDTH]
        w_qkvt_bf = w_in_bf[:, POOL_WIDTH:].T
        w_pool_bf = w_pool[l].astype(BF16)
        w_out_bf = w_out[l].astype(BF16)
        w_xq_bf, w_xo_bf = w_xq[l].astype(BF16), w_xo[l].astype(BF16)
        w_up_bf, w_down_bf = w_up[l].astype(BF16), w_down[l].astype(BF16)

        pool, u_tail, qt, kt, vt = _in_proj_prompt(hp, vec(g_pre_mix), w_u_bf, w_qkvt_bf, inv_col,
                                                   w_pool_bf, vec(pool_scale), batch)
        mk, mv = _mem_kv(mem_prompt.reshape(batch * MEM_LEN, D_MODEL), vec(g_mem),
                         w_xk[l].astype(BF16), w_xv[l].astype(BF16))
        hp = _post_prompt(pool, _moba_prompt(qt, kt, vt), hp, w_out_bf, vec(g_post_mix),
                          vec(g_pre_x), w_xq_bf, mk.reshape(batch, MEM_LEN, D_MODEL),
                          mv.reshape(batch, MEM_LEN, D_MODEL), w_xo_bf, vec(g_post_x),
                          vec(g_pre_ffn), w_up_bf, w_down_bf, vec(g_post_ffn))
        pool_p.append(u_tail[:, POOL_HALO - POOL_STATE:])
        k_p.append(kt.reshape(batch, N_HEADS, HEAD_DIM, SEQ).transpose(0, 3, 1, 2))
        v_p.append(vt.reshape(batch, N_HEADS, HEAD_DIM, SEQ).transpose(0, 3, 1, 2))
        mk_p.append(mk.reshape(batch, MEM_LEN, MEM_HEADS, MEM_HEAD_DIM))
        mv_p.append(mv.reshape(batch, MEM_LEN, MEM_HEADS, MEM_HEAD_DIM))

        u, q, k, v = _in_proj_rows(hs, vec(g_pre_mix), w_in_bf, inv_lanes, DEC_SEQ, PAST_LEN)
        u_ext = jnp.concatenate([state_pool[l], u.reshape(db, DEC_SEQ, POOL_WIDTH)], axis=1)
        pool = _pool_sample(jnp.pad(u_ext, ((0, 0), (1, 0), (0, 0))), w_pool_bf, vec(pool_scale))
        cache_kt = cache_k[l].transpose(0, 2, 3, 1).reshape(n_phys, ATT_WIDTH, PAGE_SIZE)
        cache_vt = cache_v[l].transpose(0, 2, 3, 1).reshape(n_phys, ATT_WIDTH, PAGE_SIZE)
        att = _moba_sample(q.reshape(db, DEC_SEQ, ATT_WIDTH), k.reshape(db, DEC_SEQ, ATT_WIDTH),
                           v.reshape(db, DEC_SEQ, ATT_WIDTH), cache_kt, cache_vt, page_table)
        hs = _mix_out(pool, att.reshape(db * DEC_SEQ, ATT_WIDTH), w_out_bf, vec(g_post_mix), hs)
        mem_native = lambda a: a[l].reshape(db, MEM_LEN, MEM_HEADS, MEM_CHUNKS, LANES).transpose(
            0, 1, 3, 2, 4).reshape(db, MEM_LEN, MEM_ROWS, LANES)
        hs = _xattn_sample(hs, vec(g_pre_x), w_xq_bf, mem_native(cache_mem_k), mem_native(cache_mem_v),
                           w_xo_bf, vec(g_post_x), n_mem=4)
        hs = _ffn(hs, vec(g_pre_ffn), w_up_bf, w_down_bf, vec(g_post_ffn))
        pool_s.append(u_ext[:, u_ext.shape[1] - POOL_STATE:])
        k_s.append(k.reshape(db, DEC_SEQ, N_HEADS, HEAD_DIM))
        v_s.append(v.reshape(db, DEC_SEQ, N_HEADS, HEAD_DIM))

    return (hp.reshape(batch, SEQ, D_MODEL), hs.reshape(db, DEC_SEQ, D_MODEL),
            jnp.stack(pool_p), jnp.stack(k_p), jnp.stack(v_p), jnp.stack(mk_p), jnp.stack(mv_p),
            jnp.stack(pool_s), jnp.stack(k_s), jnp.stack(v_s))
```

```python
import functools

import numpy as np
import jax
import jax.numpy as jnp
from jax import lax
from jax.experimental import pallas as pl
from jax.experimental.pallas import tpu as pltpu

F32 = jnp.float32
BF16 = jnp.bfloat16

D_MODEL = 1024
SEQ = 8192
DEC_SEQ = 8
PAST_LEN = 8192
PAGE_SIZE = 128
POOL_WIDTH = 512
POOL_WINDOWS = (2, 4, 8, 16)
POOL_GW = 128
POOL_STATE = 15
ATT_WIDTH = 512
N_HEADS = 8
HEAD_DIM = 64
ROT_DIM = 16
ROPE_THETA = 500000.0
MOBA_BLOCK = 256
MOBA_TOPK = 3
MEM_LEN = 256
MEM_HEADS = 4
MEM_HEAD_DIM = 256
D_FF = 4096
EPS = 1e-6

LANES = 128
SUBLANES = 8

ROW_TILE = 512
POOL_HALO = 16
N_PAST_BLOCKS = PAST_LEN // MOBA_BLOCK
PAGES_PER_BLOCK = MOBA_BLOCK // PAGE_SIZE
GATE_PAD = LANES
KV_SLOTS = 32
KV_UNROLL = 4
PAST_GROUP = 8
PAST_STEP = 2
PROMPT_HEADS = 2
ONES_ROWS = 16
ROT_HALF = ROT_DIM // 2
MEM_CHUNKS = MEM_HEAD_DIM // LANES
MEM_ROWS = MEM_CHUNKS * MEM_HEADS
NEG_INF = float("-inf")
MASK_BIAS = -1e30
LOG2_E = 1.4426950408889634

assert PAST_LEN % MOBA_BLOCK == 0, "sample own-block is assumed to hold new keys only"
assert PAST_LEN + 1 >= max(POOL_WINDOWS)
assert ROT_HALF == SUBLANES, "feature-major rotary assumes one sublane group per rotary half"
assert (2 * N_PAST_BLOCKS) % KV_SLOTS == 0 and N_PAST_BLOCKS % KV_UNROLL == 0 and KV_SLOTS % KV_UNROLL == 0
assert (SEQ // MOBA_BLOCK) % PAST_GROUP == 0 and HEAD_DIM + SEQ // MOBA_BLOCK <= LANES


def _params(n_grid, vmem_mib):
    return pltpu.CompilerParams(
        dimension_semantics=("arbitrary",) * n_grid,
        vmem_limit_bytes=vmem_mib * 1024 * 1024,
    )


def _rmsnorm(x, g):
    ms = jnp.mean(x * x, axis=-1, keepdims=True)
    return x * lax.rsqrt(ms + EPS) * g


def _dot(a, b):
    return jnp.dot(a, b, preferred_element_type=F32)


def _dot_nt(a, b, precision=None):
    return lax.dot_general(a, b, (((1,), (1,)), ((), ())), precision=precision,
                           preferred_element_type=F32)


def _topk_mask(gate, valid, axis):
    n = gate.shape[axis]
    idx = lax.broadcasted_iota(jnp.int32, gate.shape, axis).astype(F32)
    g = jnp.where(valid, gate, NEG_INF)
    sel = jnp.zeros(gate.shape, F32)
    for _ in range(MOBA_TOPK):
        mx = jnp.max(g, axis=axis, keepdims=True)
        first = jnp.min(jnp.where(g == mx, idx, float(n)), axis=axis, keepdims=True)
        pick = idx == first
        sel = jnp.where(pick, 1.0, sel)
        g = jnp.where(pick, NEG_INF, g)
    return jnp.where(valid, sel, 0.0)


def _rope_rows(z, pos_rows, inv_lanes):
    tm = z.shape[0]
    ang = pos_rows * inv_lanes
    cos = jnp.cos(ang)
    sin = jnp.sin(ang)
    d = lax.rem(lax.broadcasted_iota(jnp.int32, (tm, LANES), 1), HEAD_DIM)
    sin_lo = jnp.where(d < ROT_HALF, -sin, 0.0)
    sin_hi = jnp.where((d >= ROT_HALF) & (d < ROT_DIM), sin, 0.0)
    out = []
    for c in range(ATT_WIDTH // LANES):
        zz = z[:, c * LANES:(c + 1) * LANES]
        out.append(zz * cos
                   + pltpu.roll(zz, LANES - ROT_HALF, 1) * sin_lo
                   + pltpu.roll(zz, ROT_HALF, 1) * sin_hi)
    return out


def _in_proj_rows_kernel(x_ref, g_ref, w_ref, inv_ref, u_ref, q_ref, k_ref, v_ref, *, period, offset):
    tm = x_ref.shape[0]
    i = pl.program_id(0)
    n = _rmsnorm(x_ref[...], g_ref[...]).astype(BF16)

    def seg(j):
        return _dot(n, w_ref[:, j * ATT_WIDTH:(j + 1) * ATT_WIDTH])

    u_ref[...] = seg(0)
    v_ref[...] = seg(3)
    row = lax.broadcasted_iota(jnp.int32, (tm, LANES), 0) + i * tm
    pos = (lax.rem(row, period) + offset).astype(F32)
    for j, ref in ((1, q_ref), (2, k_ref)):
        for c, piece in enumerate(_rope_rows(seg(j), pos, inv_ref[...])):
            ref[:, c * LANES:(c + 1) * LANES] = piece


def _in_proj_rows(x, g, w_bf, inv_lanes, period, offset):
    m = x.shape[0]
    tm = min(ROW_TILE, m)
    out = jax.ShapeDtypeStruct((m, ATT_WIDTH), F32)
    row_spec = pl.BlockSpec((tm, ATT_WIDTH), lambda i: (i, 0))
    return pl.pallas_call(
        functools.partial(_in_proj_rows_kernel, period=period, offset=offset),
        grid=(m // tm,),
        in_specs=[
            pl.BlockSpec((tm, D_MODEL), lambda i: (i, 0)),
            pl.BlockSpec((1, D_MODEL), lambda i: (0, 0)),
            pl.BlockSpec((D_MODEL, 4 * ATT_WIDTH), lambda i: (0, 0)),
            pl.BlockSpec((1, LANES), lambda i: (0, 0)),
        ],
        out_specs=[row_spec] * 4,
        out_shape=[out] * 4,
        compiler_params=_params(1, 40),
        name="in_proj_rows",
    )(x, g, w_bf, inv_lanes)


def _in_proj_prompt_kernel(x_ref, g_ref, wu_ref, wqkvt_ref, invc_ref, wpool_ref, sc_ref,
                           pool_ref, tail_ref, qt_ref, kt_ref, vt_ref, ext_ref):
    tm = x_ref.shape[0]
    i = pl.program_id(1)
    n = _rmsnorm(x_ref[...], g_ref[...]).astype(BF16)
    u = _dot(n, wu_ref[...])

    @pl.when(i == 0)
    def _():
        ext_ref[0:POOL_HALO, :] = jnp.zeros((POOL_HALO, POOL_WIDTH), F32)

    @pl.when(i > 0)
    def _():
        ext_ref[0:POOL_HALO, :] = ext_ref[tm:tm + POOL_HALO, :]

    ext_ref[POOL_HALO:, :] = u
    tail_ref[0] = u[tm - POOL_HALO:, :]
    _pool_mix_rows(u, ext_ref, i * tm, wpool_ref, sc_ref, pool_ref)

    vt_ref[0] = _dot_nt(wqkvt_ref[2 * ATT_WIDTH:, :], n)
    pos_t = (lax.broadcasted_iota(jnp.int32, (ROT_HALF, tm), 1) + i * tm).astype(F32)
    ang = invc_ref[...] * pos_t
    cos = jnp.cos(ang)
    sin = jnp.sin(ang)
    for off, ref in ((0, qt_ref), (ATT_WIDTH, kt_ref)):
        zt = _dot_nt(wqkvt_ref[off:off + ATT_WIDTH, :], n)
        for h in range(N_HEADS):
            r0 = h * HEAD_DIM
            x1 = zt[r0:r0 + ROT_HALF, :]
            x2 = zt[r0 + ROT_HALF:r0 + ROT_DIM, :]
            ref[0, r0:r0 + ROT_HALF, :] = x1 * cos - x2 * sin
            ref[0, r0 + ROT_HALF:r0 + ROT_DIM, :] = x2 * cos + x1 * sin
            ref[0, r0 + ROT_DIM:r0 + HEAD_DIM, :] = zt[r0 + ROT_DIM:r0 + HEAD_DIM, :]


def _in_proj_prompt(x, g, wu_bf, wqkvt_bf, inv_col, w_pool_bf, pool_scale, batch):
    tm = ROW_TILE
    tiles = SEQ // tm
    t_spec = pl.BlockSpec((1, ATT_WIDTH, tm), lambda b, i: (b, 0, i))
    feat = jax.ShapeDtypeStruct((batch, ATT_WIDTH, SEQ), F32)
    return pl.pallas_call(
        _in_proj_prompt_kernel,
        grid=(batch, tiles),
        in_specs=[
            pl.BlockSpec((tm, D_MODEL), lambda b, i: (b * tiles + i, 0)),
            pl.BlockSpec((1, D_MODEL), lambda b, i: (0, 0)),
            pl.BlockSpec((D_MODEL, POOL_WIDTH), lambda b, i: (0, 0)),
            pl.BlockSpec((3 * ATT_WIDTH, D_MODEL), lambda b, i: (0, 0)),
            pl.BlockSpec((ROT_HALF, 1), lambda b, i: (0, 0)),
            pl.BlockSpec((len(POOL_WINDOWS), POOL_GW, POOL_GW), lambda b, i: (0, 0, 0)),
            pl.BlockSpec((1, POOL_WIDTH), lambda b, i: (0, 0)),
        ],
        out_specs=[pl.BlockSpec((tm, POOL_WIDTH), lambda b, i: (b * tiles + i, 0)),
                   pl.BlockSpec((1, POOL_HALO, POOL_WIDTH), lambda b, i: (b, 0, 0)),
                   t_spec, t_spec, t_spec],
        out_shape=[jax.ShapeDtypeStruct((batch * SEQ, POOL_WIDTH), F32),
                   jax.ShapeDtypeStruct((batch, POOL_HALO, POOL_WIDTH), F32), feat, feat, feat],
        scratch_shapes=[pltpu.VMEM((POOL_HALO + tm, POOL_WIDTH), F32)],
        compiler_params=_params(2, 40),
        name="in_proj_prompt",
    )(x, g, wu_bf, wqkvt_bf, inv_col, w_pool_bf, pool_scale)


def _pool_mix_rows(u, ext_ref, pos0, w_ref, sc_ref, o_ref):
    tm = u.shape[0]
    pos = pos0 + lax.broadcasted_iota(jnp.int32, (tm, POOL_GW), 0)
    for g, w in enumerate(POOL_WINDOWS):
        cols = slice(g * POOL_GW, (g + 1) * POOL_GW)
        cur = u[:, cols]
        s = cur
        for k in range(1, w):
            s = s + ext_ref[POOL_HALO - k:POOL_HALO - k + tm, cols]
        cnt = jnp.minimum(pos + 1, w).astype(F32)
        d = s / cnt - cur
        o_ref[:, cols] = _dot(d.astype(BF16), w_ref[g]) * sc_ref[:, cols]


def _pool_sample_kernel(e_ref, w_ref, sc_ref, o_ref):
    nb = e_ref.shape[0]
    lo = 1 + POOL_STATE
    t = lax.broadcasted_iota(jnp.int32, (nb, DEC_SEQ, POOL_GW), 1)
    for g, w in enumerate(POOL_WINDOWS):
        cols = slice(g * POOL_GW, (g + 1) * POOL_GW)
        cur = e_ref[:, lo:lo + DEC_SEQ, cols]
        s = cur
        for k in range(1, w):
            s = s + e_ref[:, lo - k:lo - k + DEC_SEQ, cols]
        cnt = jnp.minimum(PAST_LEN + t + 1, w).astype(F32)
        d = (s / cnt - cur).reshape(nb * DEC_SEQ, POOL_GW)
        o_ref[:, cols] = _dot(d.astype(BF16), w_ref[g]) * sc_ref[:, cols]


def _pool_sample(ext, w_pool_bf, pool_scale):
    db, rows, _ = ext.shape
    nb = 32
    return pl.pallas_call(
        _pool_sample_kernel,
        grid=(db // nb,),
        in_specs=[
            pl.BlockSpec((nb, rows, POOL_WIDTH), lambda i: (i, 0, 0)),
            pl.BlockSpec((len(POOL_WINDOWS), POOL_GW, POOL_GW), lambda i: (0, 0, 0)),
            pl.BlockSpec((1, POOL_WIDTH), lambda i: (0, 0)),
        ],
        out_specs=pl.BlockSpec((nb * DEC_SEQ, POOL_WIDTH), lambda i: (i, 0)),
        out_shape=jax.ShapeDtypeStruct((db * DEC_SEQ, POOL_WIDTH), F32),
        compiler_params=_params(1, 32),
        name="pool_sample",
    )(ext, w_pool_bf, pool_scale)


def _moba_prompt_body(qt_ref, kt_ref, vt_ref, o_ref, kaug_ref, kmean_ref, blk):
    tq = MOBA_BLOCK
    n_blocks = SEQ // MOBA_BLOCK
    pad_rows = LANES - HEAD_DIM - n_blocks

    heads = range(PROMPT_HEADS)
    head_rows = [slice(a * HEAD_DIM, (a + 1) * HEAD_DIM) for a in heads]

    @pl.when(blk == 0)
    def _():
        block_row = lax.broadcasted_iota(jnp.int32, (n_blocks, MOBA_BLOCK), 0)
        lane = lax.broadcasted_iota(jnp.int32, (1, LANES), 1)
        for a in heads:
            for n in range(n_blocks):
                kb = kt_ref[0, head_rows[a], n * MOBA_BLOCK:(n + 1) * MOBA_BLOCK]
                aug = jnp.concatenate([kb, jnp.where(block_row == n, 1.0, 0.0),
                                       jnp.zeros((pad_rows, MOBA_BLOCK), F32)], axis=0).T
                kaug_ref[a, n * MOBA_BLOCK:(n + 1) * MOBA_BLOCK, :] = aug.astype(BF16)
                kmean_ref[a, n:n + 1, :] = jnp.where(
                    lane < HEAD_DIM, jnp.sum(aug, axis=0, keepdims=True) * (1.0 / MOBA_BLOCK), 0.0)

    n_idx = lax.broadcasted_iota(jnp.int32, (n_blocks, tq), 0)
    w_past, w_own = [], []
    for a in heads:
        qt = qt_ref[0, head_rows[a], :]
        gate = jnp.dot(kmean_ref[a], jnp.concatenate([qt, jnp.zeros((LANES - HEAD_DIM, tq), F32)], axis=0),
                       precision=lax.Precision.HIGHEST, preferred_element_type=F32)
        sel = _topk_mask(gate, n_idx < blk, axis=0)
        bias = jnp.where(sel > 0.5, 0.0, MASK_BIAS)
        q_log2 = qt * (HEAD_DIM ** -0.5 * LOG2_E)
        w_past.append(jnp.concatenate([q_log2, bias, jnp.zeros((pad_rows, tq), F32)], axis=0).astype(BF16))
        w_own.append(jnp.concatenate([q_log2, jnp.zeros((LANES - HEAD_DIM, tq), F32)], axis=0).astype(BF16))

    def scores(a, w, start, size):
        return _dot(kaug_ref[a, pl.ds(pl.multiple_of(start, MOBA_BLOCK), size), :], w)

    def values(a, start, size):
        vb = vt_ref[0, head_rows[a], pl.ds(pl.multiple_of(start, MOBA_BLOCK), size)]
        return jnp.concatenate([vb.astype(BF16), jnp.ones((ONES_ROWS, size), BF16)], axis=0)

    key_pos = lax.broadcasted_iota(jnp.int32, (MOBA_BLOCK, tq), 0)
    q_pos = lax.broadcasted_iota(jnp.int32, (MOBA_BLOCK, tq), 1)
    own = [jnp.where(key_pos <= q_pos, scores(a, w_own[a], blk * MOBA_BLOCK, MOBA_BLOCK), NEG_INF) for a in heads]
    init = []
    for a in heads:
        m0 = jnp.max(own[a], axis=0, keepdims=True)
        init += [m0, _dot(values(a, blk * MOBA_BLOCK, MOBA_BLOCK), jnp.exp2(own[a] - m0).astype(BF16))]

    span = PAST_GROUP * MOBA_BLOCK
    step = PAST_STEP * MOBA_BLOCK

    def body(i, carry):
        carry = list(carry)
        starts = [i * span + u * step for u in range(PAST_GROUP // PAST_STEP)]
        tiles = [[scores(a, w_past[a], start, step) for start in starts] for a in heads]
        for u, start in enumerate(starts):
            for a in heads:
                m, acc = carry[2 * a:2 * a + 2]
                s = tiles[a][u]
                m_new = jnp.maximum(m, jnp.max(s, axis=0, keepdims=True))
                p = jnp.exp2(s - m_new).astype(BF16)
                carry[2 * a:2 * a + 2] = [m_new, jnp.exp2(m - m_new) * acc + _dot(values(a, start, step), p)]
        return tuple(carry)

    res = lax.fori_loop(0, (blk + PAST_GROUP - 1) // PAST_GROUP, body, tuple(init))
    for a in heads:
        acc = res[2 * a + 1]
        o_ref[0, head_rows[a], :] = acc[0:HEAD_DIM, :] / acc[HEAD_DIM:HEAD_DIM + 1, :]


def _moba_sample_phase(pt_ref, q_ref, kn_ref, vn_ref, ck_hbm, cv_hbm, o_ref,
                       buf, sem, s_ref, p_ref, gate_ref, b, phase, n_seq):
    nb = N_PAST_BLOCKS
    n_rows = N_HEADS * DEC_SEQ
    scale = HEAD_DIM ** -0.5

    def page_copy(cache, seq, g, half):
        slot = lax.rem(g, KV_SLOTS)
        page = pt_ref[seq, lax.rem(g, nb) * PAGES_PER_BLOCK + half]
        return pltpu.make_async_copy(
            cache.at[page],
            buf.at[slot, :, pl.ds(half * PAGE_SIZE, PAGE_SIZE)],
            sem.at[slot, half])

    def start_chunk(seq, g):
        @pl.when(g < nb)
        def _():
            for half in range(PAGES_PER_BLOCK):
                page_copy(ck_hbm, seq, g, half).start()

        @pl.when(g >= nb)
        def _():
            for half in range(PAGES_PER_BLOCK):
                page_copy(cv_hbm, seq, g, half).start()

    def start_ahead(g):
        nxt = g + KV_SLOTS

        @pl.when(nxt < 2 * nb)
        def _():
            start_chunk(b, nxt)

        @pl.when((nxt >= 2 * nb) & (b + 1 < n_seq))
        def _():
            start_chunk(b + 1, nxt - 2 * nb)

    def wait_chunk(g):
        for half in range(PAGES_PER_BLOCK):
            page_copy(ck_hbm, b, g, half).wait()

    @pl.when((b == 0) & (phase == 0))
    def _():
        for g in range(KV_SLOTS):
            start_chunk(b, jnp.int32(g))

    q = q_ref[0]
    lane_head = lax.broadcasted_iota(jnp.int32, (DEC_SEQ, ATT_WIDTH), 1) // HEAD_DIM
    qbd = jnp.concatenate([jnp.where(lane_head == h, q, 0.0) for h in range(N_HEADS)], axis=0)
    q_hi = qbd.astype(BF16)
    q_lo = (qbd - q_hi.astype(F32)).astype(BF16)
    q_stack = jnp.concatenate([q_hi, q_lo], axis=0)

    def raw_scores(stacked):
        return stacked[0:n_rows] + stacked[n_rows:]

    gate_idx = lax.broadcasted_iota(jnp.int32, (n_rows, GATE_PAD), 1)

    def k_body(i, gate):
        chunks = [i * KV_UNROLL + u for u in range(KV_UNROLL)]
        for g in chunks:
            wait_chunk(g)
        for g in chunks:
            kb = buf[lax.rem(g, KV_SLOTS)].astype(BF16)
            s = raw_scores(_dot(q_stack, kb))
            s_ref[g] = s * scale
            gate = jnp.where(gate_idx == g, jnp.sum(s, axis=1, keepdims=True) * (1.0 / MOBA_BLOCK), gate)
        for g in chunks:
            start_ahead(g)
        return gate

    @pl.when(phase == 0)
    def _():
        gate_ref[...] = lax.fori_loop(0, nb // KV_UNROLL, k_body, jnp.zeros((n_rows, GATE_PAD), F32))

    @pl.when(phase == 1)
    def _():
        sel = _topk_mask(gate_ref[...], gate_idx < nb, axis=1)

        pad = jnp.zeros((LANES - DEC_SEQ, ATT_WIDTH), F32)
        kn = jnp.concatenate([kn_ref[0], pad], axis=0).astype(BF16)
        vn = jnp.concatenate([vn_ref[0], pad], axis=0).astype(BF16)
        t_row = lax.rem(lax.broadcasted_iota(jnp.int32, (n_rows, LANES), 0), DEC_SEQ)
        t_col = lax.broadcasted_iota(jnp.int32, (n_rows, LANES), 1)
        s_own = jnp.where(t_col <= t_row, raw_scores(_dot_nt(q_stack, kn)) * scale, NEG_INF)

        m_lanes = jnp.full((n_rows, MOBA_BLOCK), NEG_INF, F32)
        for n in range(nb):
            m_lanes = jnp.maximum(m_lanes, jnp.where(sel[:, n:n + 1] > 0.5, s_ref[n], NEG_INF))
        m = jnp.maximum(jnp.max(s_own, axis=1, keepdims=True), jnp.max(m_lanes, axis=1, keepdims=True))
        p_own = jnp.exp(s_own - m)
        l_lanes = jnp.zeros((n_rows, MOBA_BLOCK), F32)
        for n in range(nb):
            pn = jnp.exp(jnp.where(sel[:, n:n + 1] > 0.5, s_ref[n], NEG_INF) - m)
            l_lanes = l_lanes + pn
            p_ref[n] = pn.astype(BF16)
        l = jnp.sum(p_own, axis=1, keepdims=True) + jnp.sum(l_lanes, axis=1, keepdims=True)

        def v_body(i, acc):
            chunks = [nb + i * KV_UNROLL + u for u in range(KV_UNROLL)]
            for g in chunks:
                wait_chunk(g)
            for g in chunks:
                vb = buf[lax.rem(g, KV_SLOTS)].astype(BF16)
                acc = acc + _dot_nt(p_ref[g - nb], vb)
            for g in chunks:
                start_ahead(g)
            return acc

        acc = lax.fori_loop(0, nb // KV_UNROLL, v_body, _dot(p_own.astype(BF16), vn))
        o = acc / l
        out = jnp.zeros((DEC_SEQ, ATT_WIDTH), F32)
        for h in range(N_HEADS):
            out = jnp.where(lane_head == h, o[h * DEC_SEQ:(h + 1) * DEC_SEQ, :], out)
        o_ref[0] = out


def _moba_kernel(pt_ref, qt_ref, kt_ref, vt_ref, sq_ref, skn_ref, svn_ref, ck_hbm, cv_hbm, o_ref, so_ref,
                 kaug_ref, kmean_ref, buf, sem, s_ref, p_ref, gate_ref):
    blk = pl.program_id(2)
    step = (pl.program_id(0) * pl.num_programs(1) + pl.program_id(1)) * pl.num_programs(2) + blk
    n_steps = pl.num_programs(0) * pl.num_programs(1) * pl.num_programs(2)
    _moba_prompt_body(qt_ref, kt_ref, vt_ref, o_ref, kaug_ref, kmean_ref, blk)
    _moba_sample_phase(pt_ref, sq_ref, skn_ref, svn_ref, ck_hbm, cv_hbm, so_ref, buf, sem, s_ref, p_ref, gate_ref,
                       step // 2, lax.rem(step, 2), n_steps // 2)


def _moba(qt, kt, vt, q_s, k_new, v_new, cache_kt, cache_vt, page_table):
    b = qt.shape[0]
    db = q_s.shape[0]
    n_hp = N_HEADS // PROMPT_HEADS
    n_qi = SEQ // MOBA_BLOCK
    assert b * n_hp * n_qi == 2 * db, "two grid steps per sample sequence"
    rows = PROMPT_HEADS * HEAD_DIM
    n_rows = N_HEADS * DEC_SEQ
    kv_spec = pl.BlockSpec((1, rows, SEQ), lambda bi, h, qi, pt: (bi, h, 0))
    q_spec = pl.BlockSpec((1, rows, MOBA_BLOCK), lambda bi, h, qi, pt: (bi, h, qi))
    tok_spec = pl.BlockSpec((1, DEC_SEQ, ATT_WIDTH), lambda bi, h, qi, pt: (((bi * n_hp + h) * n_qi + qi) // 2, 0, 0))
    any_spec = pl.BlockSpec(memory_space=pl.ANY)
    grid_spec = pltpu.PrefetchScalarGridSpec(
        num_scalar_prefetch=1,
        grid=(b, n_hp, n_qi),
        in_specs=[q_spec, kv_spec, kv_spec, tok_spec, tok_spec, tok_spec, any_spec, any_spec],
        out_specs=[q_spec, tok_spec],
        scratch_shapes=[
            pltpu.VMEM((PROMPT_HEADS, SEQ, LANES), BF16),
            pltpu.VMEM((PROMPT_HEADS, SEQ // MOBA_BLOCK, LANES), F32),
            pltpu.VMEM((KV_SLOTS, ATT_WIDTH, MOBA_BLOCK), F32),
            pltpu.SemaphoreType.DMA((KV_SLOTS, PAGES_PER_BLOCK)),
            pltpu.VMEM((N_PAST_BLOCKS, n_rows, MOBA_BLOCK), F32),
            pltpu.VMEM((N_PAST_BLOCKS, n_rows, MOBA_BLOCK), BF16),
            pltpu.VMEM((n_rows, GATE_PAD), F32),
        ],
    )
    return pl.pallas_call(
        _moba_kernel,
        grid_spec=grid_spec,
        out_shape=[jax.ShapeDtypeStruct((b, ATT_WIDTH, SEQ), F32),
                   jax.ShapeDtypeStruct((db, DEC_SEQ, ATT_WIDTH), F32)],
        compiler_params=_params(3, 58),
        name="moba",
    )(page_table, qt, kt, vt, q_s, k_new, v_new, cache_kt, cache_vt)


def _mix_out_rows(pool, att, w_ref, g, res):
    y = _dot(pool.astype(BF16), w_ref[0:POOL_WIDTH, :])
    y = y + _dot(att.astype(BF16), w_ref[POOL_WIDTH:, :])
    return res + _rmsnorm(y, g)


def _mix_out_kernel(a_ref, b_ref, w_ref, g_ref, r_ref, o_ref):
    o_ref[...] = _mix_out_rows(a_ref[...], b_ref[...], w_ref, g_ref[...], r_ref[...])


def _mix_out(pool, att, w_bf, g, res):
    m = res.shape[0]
    tm = min(ROW_TILE, m)
    half_spec = pl.BlockSpec((tm, POOL_WIDTH), lambda i: (i, 0))
    row_spec = pl.BlockSpec((tm, D_MODEL), lambda i: (i, 0))
    return pl.pallas_call(
        _mix_out_kernel,
        grid=(m // tm,),
        in_specs=[half_spec, half_spec,
                  pl.BlockSpec((D_MODEL, D_MODEL), lambda i: (0, 0)),
                  pl.BlockSpec((1, D_MODEL), lambda i: (0, 0)),
                  row_spec],
        out_specs=row_spec,
        out_shape=jax.ShapeDtypeStruct((m, D_MODEL), F32),
        compiler_params=_params(1, 32),
        name="mix_out",
    )(pool, att, w_bf, g, res)


def _mem_kv_kernel(x_ref, g_ref, wk_ref, wv_ref, k_ref, v_ref):
    n = _rmsnorm(x_ref[...], g_ref[...]).astype(BF16)
    k_ref[...] = _dot(n, wk_ref[...])
    v_ref[...] = _dot(n, wv_ref[...])


def _mem_kv(mem, g, wk_bf, wv_bf):
    m = mem.shape[0]
    tm = MEM_LEN
    row_spec = pl.BlockSpec((tm, D_MODEL), lambda i: (i, 0))
    w_spec = pl.BlockSpec((D_MODEL, D_MODEL), lambda i: (0, 0))
    out = jax.ShapeDtypeStruct((m, D_MODEL), F32)
    return pl.pallas_call(
        _mem_kv_kernel,
        grid=(m // tm,),
        in_specs=[row_spec, pl.BlockSpec((1, D_MODEL), lambda i: (0, 0)), w_spec, w_spec],
        out_specs=[row_spec, row_spec],
        out_shape=[out, out],
        compiler_params=_params(1, 32),
        name="mem_kv",
    )(mem, g, wk_bf, wv_bf)


def _xattn_rows(h, gpre, wq_ref, mk_ref, mv_ref, wo_ref, gpost):
    scale = MEM_HEAD_DIM ** -0.5
    q = _dot(_rmsnorm(h, gpre).astype(BF16), wq_ref[...]).astype(BF16)
    y = jnp.zeros(h.shape, F32)
    for hh in range(MEM_HEADS):
        cols = slice(hh * MEM_HEAD_DIM, (hh + 1) * MEM_HEAD_DIM)
        s = _dot_nt(q[:, cols], mk_ref[0, :, cols].astype(BF16)) * scale
        p = jnp.exp(s - jnp.max(s, axis=-1, keepdims=True))
        p = (p / jnp.sum(p, axis=-1, keepdims=True)).astype(BF16)
        oh = _dot(p, mv_ref[0, :, cols].astype(BF16))
        y = y + _dot(oh.astype(BF16), wo_ref[cols, :])
    return h + _rmsnorm(y, gpost)

def _xattn_sample_kernel(h_ref, gpre_ref, wq_ref, mk_ref, mv_ref, wo_ref, gpost_ref, o_ref):
    tm = h_ref.shape[0]
    n_mem = mk_ref.shape[0]
    t_rows = tm // n_mem
    width = MEM_LEN * MEM_ROWS
    n_q = MEM_HEADS * t_rows
    scale = MEM_HEAD_DIM ** -0.5
    h = h_ref[...]
    q = _dot(_rmsnorm(h, gpre_ref[...]).astype(BF16), wq_ref[...]).astype(BF16)
    col = lax.broadcasted_iota(jnp.int32, (n_q, width), 1)
    row_head = lax.broadcasted_iota(jnp.int32, (n_q, width), 0) // t_rows
    own = (lax.rem(col, MEM_ROWS) == row_head)
    rows = []
    for b in range(n_mem):
        keys = mk_ref[b].reshape(width, LANES).astype(BF16)
        vals = mv_ref[b].reshape(width, LANES).astype(BF16)
        qb = q[b * t_rows:(b + 1) * t_rows, :]
        q_all = jnp.concatenate([qb[:, hh * MEM_HEAD_DIM + c * LANES:hh * MEM_HEAD_DIM + (c + 1) * LANES]
                                 for c in range(MEM_CHUNKS) for hh in range(MEM_HEADS)], axis=0)
        s_all = _dot_nt(q_all, keys)
        s = s_all[0:n_q]
        for c in range(1, MEM_CHUNKS):
            s = s + pltpu.roll(s_all[c * n_q:(c + 1) * n_q], width - c * MEM_HEADS, 1)
        s = jnp.where(own, s * scale, NEG_INF)
        p = jnp.exp(s - jnp.max(s, axis=-1, keepdims=True))
        p = p / jnp.sum(p, axis=-1, keepdims=True)
        p_all = jnp.concatenate([p] + [pltpu.roll(p, c * MEM_HEADS, 1) for c in range(1, MEM_CHUNKS)], axis=0)
        o_all = _dot(p_all.astype(BF16), vals)
        rows.append(jnp.concatenate(
            [o_all[(c * MEM_HEADS + hh) * t_rows:(c * MEM_HEADS + hh + 1) * t_rows, :]
             for hh in range(MEM_HEADS) for c in range(MEM_CHUNKS)], axis=1))
    o = jnp.concatenate(rows, axis=0)
    o_ref[...] = h + _rmsnorm(_dot(o.astype(BF16), wo_ref[...]), gpost_ref[...])


def _xattn_sample(h, gpre, wq_bf, mk, mv, wo_bf, gpost, n_mem):
    m = h.shape[0]
    tm = n_mem * (m // mk.shape[0])
    row_spec = pl.BlockSpec((tm, D_MODEL), lambda i: (i, 0))
    vec_spec = pl.BlockSpec((1, D_MODEL), lambda i: (0, 0))
    w_spec = pl.BlockSpec((D_MODEL, D_MODEL), lambda i: (0, 0))
    mem_spec = pl.BlockSpec((n_mem, MEM_LEN, MEM_ROWS, LANES), lambda i: (i, 0, 0, 0))
    return pl.pallas_call(
        _xattn_sample_kernel,
        grid=(m // tm,),
        in_specs=[row_spec, vec_spec, w_spec, mem_spec, mem_spec, w_spec, vec_spec],
        out_specs=row_spec,
        out_shape=jax.ShapeDtypeStruct((m, D_MODEL), F32),
        compiler_params=_params(1, 48),
        name="xattn_sample",
    )(h, gpre, wq_bf, mk, mv, wo_bf, gpost)


def _ffn_rows(h, gpre, wu_ref, wd_ref, gpost):
    n = _rmsnorm(h, gpre).astype(BF16)
    y = jnp.zeros(h.shape, F32)
    chunk = D_MODEL
    for c in range(D_FF // chunk):
        a = jnp.square(jnp.maximum(_dot(n, wu_ref[:, c * chunk:(c + 1) * chunk]), 0.0))
        y = y + _dot(a.astype(BF16), wd_ref[c * chunk:(c + 1) * chunk, :])
    return h + _rmsnorm(y, gpost)


def _ffn_kernel(h_ref, gpre_ref, wu_ref, wd_ref, gpost_ref, o_ref):
    o_ref[...] = _ffn_rows(h_ref[...], gpre_ref[...], wu_ref, wd_ref, gpost_ref[...])


def _post_prompt_kernel(pool_ref, att_ref, x_ref, wout_ref, gmix_ref, gprex_ref, wq_ref, mk_ref, mv_ref, wo_ref,
                        gpostx_ref, gpreffn_ref, wu_ref, wd_ref, gpostffn_ref, o_ref):
    h = _mix_out_rows(pool_ref[...], att_ref[0].T, wout_ref, gmix_ref[...], x_ref[...])
    h = _xattn_rows(h, gprex_ref[...], wq_ref, mk_ref, mv_ref, wo_ref, gpostx_ref[...])
    o_ref[...] = _ffn_rows(h, gpreffn_ref[...], wu_ref, wd_ref, gpostffn_ref[...])


def _post_prompt(pool, att_t, x, w_out_bf, g_mix, g_pre_x, wq_bf, mk, mv, wo_bf, g_post_x,
                 g_pre_ffn, wu_bf, wd_bf, g_post_ffn):
    m = x.shape[0]
    tm = ROW_TILE
    tiles = att_t.shape[2] // tm

    def resident(shape):
        return pl.BlockSpec(shape, lambda i: (0,) * len(shape), pipeline_mode=pl.Buffered(1))

    row_spec = pl.BlockSpec((tm, D_MODEL), lambda i: (i, 0))
    vec_spec = resident((1, D_MODEL))
    sq_spec = resident((D_MODEL, D_MODEL))
    mem_spec = pl.BlockSpec((1, MEM_LEN, D_MODEL), lambda i: (i // tiles, 0, 0))
    return pl.pallas_call(
        _post_prompt_kernel,
        grid=(m // tm,),
        in_specs=[pl.BlockSpec((tm, POOL_WIDTH), lambda i: (i, 0)),
                  pl.BlockSpec((1, ATT_WIDTH, tm), lambda i: (i // tiles, 0, i % tiles)),
                  row_spec, sq_spec, vec_spec, vec_spec, sq_spec, mem_spec, mem_spec, sq_spec, vec_spec,
                  vec_spec, resident((D_MODEL, D_FF)), resident((D_FF, D_MODEL)), vec_spec],
        out_specs=row_spec,
        out_shape=jax.ShapeDtypeStruct((m, D_MODEL), F32),
        compiler_params=_params(1, 58),
        name="post_prompt",
    )(pool, att_t, x, w_out_bf, g_mix, g_pre_x, wq_bf, mk, mv, wo_bf, g_post_x, g_pre_ffn, wu_bf, wd_bf, g_post_ffn)


def _ffn(h, gpre, wu_bf, wd_bf, gpost):
    m = h.shape[0]
    tm = min(ROW_TILE, m)
    row_spec = pl.BlockSpec((tm, D_MODEL), lambda i: (i, 0))
    vec_spec = pl.BlockSpec((1, D_MODEL), lambda i: (0, 0))
    return pl.pallas_call(
        _ffn_kernel,
        grid=(m // tm,),
        in_specs=[row_spec, vec_spec,
                  pl.BlockSpec((D_MODEL, D_FF), lambda i: (0, 0)),
                  pl.BlockSpec((D_FF, D_MODEL), lambda i: (0, 0)),
                  vec_spec],
        out_specs=row_spec,
        out_shape=jax.ShapeDtypeStruct((m, D_MODEL), F32),
        compiler_params=_params(1, 56),
        name="ffn",
    )(h, gpre, wu_bf, wd_bf, gpost)


def _rope_inv():
    inv = ROPE_THETA ** (-2.0 * jnp.arange(ROT_HALF, dtype=F32) / ROT_DIM)
    d = np.arange(LANES) % HEAD_DIM
    rotary = (d < ROT_DIM).astype(np.float32)
    inv_lanes = (inv[d % ROT_HALF] * rotary).reshape(1, LANES)
    return inv_lanes, inv.reshape(ROT_HALF, 1)


def kernel(x_prompt, x_sample, mem_prompt, cache_k, cache_v, cache_mem_k, cache_mem_v, state_pool, page_table,
           g_pre_mix, w_in, w_pool, pool_scale, w_out, g_post_mix,
           g_mem, g_pre_x, w_xq, w_xk, w_xv, w_xo, g_post_x,
           g_pre_ffn, w_up, w_down, g_post_ffn):
    depth = w_in.shape[0]
    batch = x_prompt.shape[0]
    db = x_sample.shape[0]
    n_phys = cache_k.shape[1]
    inv_lanes, inv_col = _rope_inv()

    hp = x_prompt.reshape(batch * SEQ, D_MODEL)
    hs = x_sample.reshape(db * DEC_SEQ, D_MODEL)
    pool_p, k_p, v_p, mk_p, mv_p, pool_s, k_s, v_s = ([] for _ in range(8))
    for l in range(depth):
        vec = lambda a: a[l].reshape(1, -1)
        w_in_bf = w_in[l].astype(BF16)
        w_u_bf = w_in_bf[:, :POOL_WIDTH]
        w_qkvt_bf = w_in_bf[:, POOL_WIDTH:].T
        w_pool_bf = w_pool[l].astype(BF16)
        w_out_bf = w_out[l].astype(BF16)
        w_xq_bf, w_xo_bf = w_xq[l].astype(BF16), w_xo[l].astype(BF16)
        w_up_bf, w_down_bf = w_up[l].astype(BF16), w_down[l].astype(BF16)

        pool, u_tail, qt, kt, vt = _in_proj_prompt(hp, vec(g_pre_mix), w_u_bf, w_qkvt_bf, inv_col,
                                                   w_pool_bf, vec(pool_scale), batch)
        u, q, k, v = _in_proj_rows(hs, vec(g_pre_mix), w_in_bf, inv_lanes, DEC_SEQ, PAST_LEN)
        cache_kt = cache_k[l].transpose(0, 2, 3, 1).reshape(n_phys, ATT_WIDTH, PAGE_SIZE)
        cache_vt = cache_v[l].transpose(0, 2, 3, 1).reshape(n_phys, ATT_WIDTH, PAGE_SIZE)
        att_p, att_s = _moba(qt, kt, vt, q.reshape(db, DEC_SEQ, ATT_WIDTH), k.reshape(db, DEC_SEQ, ATT_WIDTH),
                             v.reshape(db, DEC_SEQ, ATT_WIDTH), cache_kt, cache_vt, page_table)

        mk, mv = _mem_kv(mem_prompt.reshape(batch * MEM_LEN, D_MODEL), vec(g_mem),
                         w_xk[l].astype(BF16), w_xv[l].astype(BF16))
        hp = _post_prompt(pool, att_p, hp, w_out_bf, vec(g_post_mix),
                          vec(g_pre_x), w_xq_bf, mk.reshape(batch, MEM_LEN, D_MODEL),
                          mv.reshape(batch, MEM_LEN, D_MODEL), w_xo_bf, vec(g_post_x),
                          vec(g_pre_ffn), w_up_bf, w_down_bf, vec(g_post_ffn))
        pool_p.append(u_tail[:, POOL_HALO - POOL_STATE:])
        k_p.append(kt.reshape(batch, N_HEADS, HEAD_DIM, SEQ).transpose(0, 3, 1, 2))
        v_p.append(vt.reshape(batch, N_HEADS, HEAD_DIM, SEQ).transpose(0, 3, 1, 2))
        mk_p.append(mk.reshape(batch, MEM_LEN, MEM_HEADS, MEM_HEAD_DIM))
        mv_p.append(mv.reshape(batch, MEM_LEN, MEM_HEADS, MEM_HEAD_DIM))

        u_ext = jnp.concatenate([state_pool[l], u.reshape(db, DEC_SEQ, POOL_WIDTH)], axis=1)
        pool = _pool_sample(jnp.pad(u_ext, ((0, 0), (1, 0), (0, 0))), w_pool_bf, vec(pool_scale))
        hs = _mix_out(pool, att_s.reshape(db * DEC_SEQ, ATT_WIDTH), w_out_bf, vec(g_post_mix), hs)
        mem_native = lambda a: a[l].reshape(db, MEM_LEN, MEM_HEADS, MEM_CHUNKS, LANES).transpose(
            0, 1, 3, 2, 4).reshape(db, MEM_LEN, MEM_ROWS, LANES)
        hs = _xattn_sample(hs, vec(g_pre_x), w_xq_bf, mem_native(cache_mem_k), mem_native(cache_mem_v),
                           w_xo_bf, vec(g_post_x), n_mem=4)
        hs = _ffn(hs, vec(g_pre_ffn), w_up_bf, w_down_bf, vec(g_post_ffn))
        pool_s.append(u_ext[:, u_ext.shape[1] - POOL_STATE:])
        k_s.append(k.reshape(db, DEC_SEQ, N_HEADS, HEAD_DIM))
        v_s.append(v.reshape(db, DEC_SEQ, N_HEADS, HEAD_DIM))

    return (hp.reshape(batch, SEQ, D_MODEL), hs.reshape(db, DEC_SEQ, D_MODEL),
            jnp.stack(pool_p), jnp.stack(k_p), jnp.stack(v_p), jnp.stack(mk_p), jnp.stack(mv_p),
            jnp.stack(pool_s), jnp.stack(k_s), jnp.stack(v_s))
```

```python
import functools

import numpy as np
import jax
import jax.numpy as jnp
from jax import lax
from jax.experimental import pallas as pl
from jax.experimental.pallas import tpu as pltpu

F32 = jnp.float32
BF16 = jnp.bfloat16

D_MODEL = 1024
SEQ = 8192
DEC_SEQ = 8
PAST_LEN = 8192
PAGE_SIZE = 128
POOL_WIDTH = 512
POOL_WINDOWS = (2, 4, 8, 16)
POOL_GW = 128
POOL_STATE = 15
ATT_WIDTH = 512
N_HEADS = 8
HEAD_DIM = 64
ROT_DIM = 16
ROPE_THETA = 500000.0
MOBA_BLOCK = 256
MOBA_TOPK = 3
MEM_LEN = 256
MEM_HEADS = 4
MEM_HEAD_DIM = 256
D_FF = 4096
EPS = 1e-6

LANES = 128
SUBLANES = 8

ROW_TILE = 512
POOL_HALO = 16
N_PAST_BLOCKS = PAST_LEN // MOBA_BLOCK
PAGES_PER_BLOCK = MOBA_BLOCK // PAGE_SIZE
GATE_PAD = LANES
KV_SLOTS = 32
KV_UNROLL = 4
PAST_GROUP = 8
PAST_STEP = 2
PAST_TAIL = 4
PROMPT_HEADS = 4
ONES_ROWS = 16
ROT_HALF = ROT_DIM // 2
MEM_CHUNKS = MEM_HEAD_DIM // LANES
MEM_ROWS = MEM_CHUNKS * MEM_HEADS
NEG_INF = float("-inf")
MASK_BIAS = -1e30
LOG2_E = 1.4426950408889634

assert PAST_LEN % MOBA_BLOCK == 0, "sample own-block is assumed to hold new keys only"
assert PAST_LEN + 1 >= max(POOL_WINDOWS)
assert ROT_HALF == SUBLANES, "feature-major rotary assumes one sublane group per rotary half"
assert (2 * N_PAST_BLOCKS) % KV_SLOTS == 0 and N_PAST_BLOCKS % KV_UNROLL == 0 and KV_SLOTS % KV_UNROLL == 0
assert (SEQ // MOBA_BLOCK) % PAST_GROUP == 0 and HEAD_DIM + SEQ // MOBA_BLOCK <= LANES
assert PAST_GROUP % PAST_TAIL == 0 and PAST_TAIL % PAST_STEP == 0


def _params(n_grid, vmem_mib):
    return pltpu.CompilerParams(
        dimension_semantics=("arbitrary",) * n_grid,
        vmem_limit_bytes=vmem_mib * 1024 * 1024,
    )


def _rmsnorm(x, g):
    ms = jnp.mean(x * x, axis=-1, keepdims=True)
    return x * lax.rsqrt(ms + EPS) * g


def _dot(a, b):
    return jnp.dot(a, b, preferred_element_type=F32)


def _dot_nt(a, b, precision=None):
    return lax.dot_general(a, b, (((1,), (1,)), ((), ())), precision=precision,
                           preferred_element_type=F32)


def _topk_mask(gate, valid, axis):
    n = gate.shape[axis]
    idx = lax.broadcasted_iota(jnp.int32, gate.shape, axis).astype(F32)
    g = jnp.where(valid, gate, NEG_INF)
    sel = jnp.zeros(gate.shape, F32)
    for _ in range(MOBA_TOPK):
        mx = jnp.max(g, axis=axis, keepdims=True)
        first = jnp.min(jnp.where(g == mx, idx, float(n)), axis=axis, keepdims=True)
        pick = idx == first
        sel = jnp.where(pick, 1.0, sel)
        g = jnp.where(pick, NEG_INF, g)
    return jnp.where(valid, sel, 0.0)


def _rope_rows(z, pos_rows, inv_lanes):
    tm = z.shape[0]
    ang = pos_rows * inv_lanes
    cos = jnp.cos(ang)
    sin = jnp.sin(ang)
    d = lax.rem(lax.broadcasted_iota(jnp.int32, (tm, LANES), 1), HEAD_DIM)
    sin_lo = jnp.where(d < ROT_HALF, -sin, 0.0)
    sin_hi = jnp.where((d >= ROT_HALF) & (d < ROT_DIM), sin, 0.0)
    out = []
    for c in range(ATT_WIDTH // LANES):
        zz = z[:, c * LANES:(c + 1) * LANES]
        out.append(zz * cos
                   + pltpu.roll(zz, LANES - ROT_HALF, 1) * sin_lo
                   + pltpu.roll(zz, ROT_HALF, 1) * sin_hi)
    return out


def _in_proj_rows_kernel(x_ref, g_ref, w_ref, inv_ref, u_ref, q_ref, k_ref, v_ref, *, period, offset):
    tm = x_ref.shape[0]
    i = pl.program_id(0)
    n = _rmsnorm(x_ref[...], g_ref[...]).astype(BF16)

    def seg(j):
        return _dot(n, w_ref[:, j * ATT_WIDTH:(j + 1) * ATT_WIDTH])

    u_ref[...] = seg(0)
    v_ref[...] = seg(3)
    row = lax.broadcasted_iota(jnp.int32, (tm, LANES), 0) + i * tm
    pos = (lax.rem(row, period) + offset).astype(F32)
    for j, ref in ((1, q_ref), (2, k_ref)):
        for c, piece in enumerate(_rope_rows(seg(j), pos, inv_ref[...])):
            ref[:, c * LANES:(c + 1) * LANES] = piece


def _in_proj_rows(x, g, w_bf, inv_lanes, period, offset):
    m = x.shape[0]
    tm = min(ROW_TILE, m)
    out = jax.ShapeDtypeStruct((m, ATT_WIDTH), F32)
    row_spec = pl.BlockSpec((tm, ATT_WIDTH), lambda i: (i, 0))
    return pl.pallas_call(
        functools.partial(_in_proj_rows_kernel, period=period, offset=offset),
        grid=(m // tm,),
        in_specs=[
            pl.BlockSpec((tm, D_MODEL), lambda i: (i, 0)),
            pl.BlockSpec((1, D_MODEL), lambda i: (0, 0)),
            pl.BlockSpec((D_MODEL, 4 * ATT_WIDTH), lambda i: (0, 0)),
            pl.BlockSpec((1, LANES), lambda i: (0, 0)),
        ],
        out_specs=[row_spec] * 4,
        out_shape=[out] * 4,
        compiler_params=_params(1, 40),
        name="in_proj_rows",
    )(x, g, w_bf, inv_lanes)


def _in_proj_prompt_kernel(x_ref, g_ref, wu_ref, wqkvt_ref, invc_ref, wpool_ref, sc_ref,
                           pool_ref, tail_ref, qt_ref, kt_ref, vt_ref, ext_ref):
    tm = x_ref.shape[0]
    i = pl.program_id(1)
    n = _rmsnorm(x_ref[...], g_ref[...]).astype(BF16)
    u = _dot(n, wu_ref[...])

    @pl.when(i == 0)
    def _():
        ext_ref[0:POOL_HALO, :] = jnp.zeros((POOL_HALO, POOL_WIDTH), F32)

    @pl.when(i > 0)
    def _():
        ext_ref[0:POOL_HALO, :] = ext_ref[tm:tm + POOL_HALO, :]

    ext_ref[POOL_HALO:, :] = u
    tail_ref[0] = u[tm - POOL_HALO:, :]
    _pool_mix_rows(u, ext_ref, i * tm, wpool_ref, sc_ref, pool_ref)

    vt_ref[0] = _dot_nt(wqkvt_ref[2 * ATT_WIDTH:, :], n)
    pos_t = (lax.broadcasted_iota(jnp.int32, (ROT_HALF, tm), 1) + i * tm).astype(F32)
    ang = invc_ref[...] * pos_t
    cos = jnp.cos(ang)
    sin = jnp.sin(ang)
    for off, ref in ((0, qt_ref), (ATT_WIDTH, kt_ref)):
        zt = _dot_nt(wqkvt_ref[off:off + ATT_WIDTH, :], n)
        for h in range(N_HEADS):
            r0 = h * HEAD_DIM
            x1 = zt[r0:r0 + ROT_HALF, :]
            x2 = zt[r0 + ROT_HALF:r0 + ROT_DIM, :]
            ref[0, r0:r0 + ROT_HALF, :] = x1 * cos - x2 * sin
            ref[0, r0 + ROT_HALF:r0 + ROT_DIM, :] = x2 * cos + x1 * sin
            ref[0, r0 + ROT_DIM:r0 + HEAD_DIM, :] = zt[r0 + ROT_DIM:r0 + HEAD_DIM, :]


def _in_proj_prompt(x, g, wu_bf, wqkvt_bf, inv_col, w_pool_bf, pool_scale, batch):
    tm = ROW_TILE
    tiles = SEQ // tm
    t_spec = pl.BlockSpec((1, ATT_WIDTH, tm), lambda b, i: (b, 0, i))
    feat = jax.ShapeDtypeStruct((batch, ATT_WIDTH, SEQ), F32)
    return pl.pallas_call(
        _in_proj_prompt_kernel,
        grid=(batch, tiles),
        in_specs=[
            pl.BlockSpec((tm, D_MODEL), lambda b, i: (b * tiles + i, 0)),
            pl.BlockSpec((1, D_MODEL), lambda b, i: (0, 0)),
            pl.BlockSpec((D_MODEL, POOL_WIDTH), lambda b, i: (0, 0)),
            pl.BlockSpec((3 * ATT_WIDTH, D_MODEL), lambda b, i: (0, 0)),
            pl.BlockSpec((ROT_HALF, 1), lambda b, i: (0, 0)),
            pl.BlockSpec((len(POOL_WINDOWS), POOL_GW, POOL_GW), lambda b, i: (0, 0, 0)),
            pl.BlockSpec((1, POOL_WIDTH), lambda b, i: (0, 0)),
        ],
        out_specs=[pl.BlockSpec((tm, POOL_WIDTH), lambda b, i: (b * tiles + i, 0)),
                   pl.BlockSpec((1, POOL_HALO, POOL_WIDTH), lambda b, i: (b, 0, 0)),
                   t_spec, t_spec, t_spec],
        out_shape=[jax.ShapeDtypeStruct((batch * SEQ, POOL_WIDTH), F32),
                   jax.ShapeDtypeStruct((batch, POOL_HALO, POOL_WIDTH), F32), feat, feat, feat],
        scratch_shapes=[pltpu.VMEM((POOL_HALO + tm, POOL_WIDTH), F32)],
        compiler_params=_params(2, 40),
        name="in_proj_prompt",
    )(x, g, wu_bf, wqkvt_bf, inv_col, w_pool_bf, pool_scale)


def _pool_mix_rows(u, ext_ref, pos0, w_ref, sc_ref, o_ref):
    tm = u.shape[0]
    pos = pos0 + lax.broadcasted_iota(jnp.int32, (tm, POOL_GW), 0)
    for g, w in enumerate(POOL_WINDOWS):
        cols = slice(g * POOL_GW, (g + 1) * POOL_GW)
        cur = u[:, cols]
        s = cur
        for k in range(1, w):
            s = s + ext_ref[POOL_HALO - k:POOL_HALO - k + tm, cols]
        cnt = jnp.minimum(pos + 1, w).astype(F32)
        d = s / cnt - cur
        o_ref[:, cols] = _dot(d.astype(BF16), w_ref[g]) * sc_ref[:, cols]


def _pool_sample_kernel(e_ref, w_ref, sc_ref, o_ref):
    nb = e_ref.shape[0]
    lo = 1 + POOL_STATE
    t = lax.broadcasted_iota(jnp.int32, (nb, DEC_SEQ, POOL_GW), 1)
    for g, w in enumerate(POOL_WINDOWS):
        cols = slice(g * POOL_GW, (g + 1) * POOL_GW)
        cur = e_ref[:, lo:lo + DEC_SEQ, cols]
        s = cur
        for k in range(1, w):
            s = s + e_ref[:, lo - k:lo - k + DEC_SEQ, cols]
        cnt = jnp.minimum(PAST_LEN + t + 1, w).astype(F32)
        d = (s / cnt - cur).reshape(nb * DEC_SEQ, POOL_GW)
        o_ref[:, cols] = _dot(d.astype(BF16), w_ref[g]) * sc_ref[:, cols]


def _pool_sample(ext, w_pool_bf, pool_scale):
    db, rows, _ = ext.shape
    nb = 32
    return pl.pallas_call(
        _pool_sample_kernel,
        grid=(db // nb,),
        in_specs=[
            pl.BlockSpec((nb, rows, POOL_WIDTH), lambda i: (i, 0, 0)),
            pl.BlockSpec((len(POOL_WINDOWS), POOL_GW, POOL_GW), lambda i: (0, 0, 0)),
            pl.BlockSpec((1, POOL_WIDTH), lambda i: (0, 0)),
        ],
        out_specs=pl.BlockSpec((nb * DEC_SEQ, POOL_WIDTH), lambda i: (i, 0)),
        out_shape=jax.ShapeDtypeStruct((db * DEC_SEQ, POOL_WIDTH), F32),
        compiler_params=_params(1, 32),
        name="pool_sample",
    )(ext, w_pool_bf, pool_scale)


def _moba_prompt_kernel(qt_ref, kt_ref, vt_ref, o_ref, kaug_ref, kmean_ref):
    tq = MOBA_BLOCK
    blk = pl.program_id(2)
    n_blocks = SEQ // MOBA_BLOCK
    pad_rows = LANES - HEAD_DIM - n_blocks

    heads = range(PROMPT_HEADS)
    head_rows = [slice(a * HEAD_DIM, (a + 1) * HEAD_DIM) for a in heads]

    @pl.when(blk == 0)
    def _():
        block_row = lax.broadcasted_iota(jnp.int32, (n_blocks, MOBA_BLOCK), 0)
        lane = lax.broadcasted_iota(jnp.int32, (1, LANES), 1)
        for a in heads:
            for n in range(n_blocks):
                kb = kt_ref[0, head_rows[a], n * MOBA_BLOCK:(n + 1) * MOBA_BLOCK]
                aug = jnp.concatenate([kb, jnp.where(block_row == n, 1.0, 0.0),
                                       jnp.zeros((pad_rows, MOBA_BLOCK), F32)], axis=0).T
                kaug_ref[a, n * MOBA_BLOCK:(n + 1) * MOBA_BLOCK, :] = aug.astype(BF16)
                kmean_ref[a, n:n + 1, :] = jnp.where(
                    lane < HEAD_DIM, jnp.sum(aug, axis=0, keepdims=True) * (1.0 / MOBA_BLOCK), 0.0)

    n_idx = lax.broadcasted_iota(jnp.int32, (n_blocks, tq), 0)
    w_past, w_own = [], []
    for a in heads:
        qt = qt_ref[0, head_rows[a], :]
        gate = jnp.dot(kmean_ref[a], jnp.concatenate([qt, jnp.zeros((LANES - HEAD_DIM, tq), F32)], axis=0),
                       precision=lax.Precision.HIGHEST, preferred_element_type=F32)
        sel = _topk_mask(gate, n_idx < blk, axis=0)
        bias = jnp.where(sel > 0.5, 0.0, MASK_BIAS)
        q_log2 = qt * (HEAD_DIM ** -0.5 * LOG2_E)
        w_past.append(jnp.concatenate([q_log2, bias, jnp.zeros((pad_rows, tq), F32)], axis=0).astype(BF16))
        w_own.append(jnp.concatenate([q_log2, jnp.zeros((LANES - HEAD_DIM, tq), F32)], axis=0).astype(BF16))

    def scores(a, w, start, size):
        return _dot(kaug_ref[a, pl.ds(pl.multiple_of(start, MOBA_BLOCK), size), :], w)

    def values(a, start, size):
        vb = vt_ref[0, head_rows[a], pl.ds(pl.multiple_of(start, MOBA_BLOCK), size)]
        return jnp.concatenate([vb.astype(BF16), jnp.ones((ONES_ROWS, size), BF16)], axis=0)

    key_pos = lax.broadcasted_iota(jnp.int32, (MOBA_BLOCK, tq), 0)
    q_pos = lax.broadcasted_iota(jnp.int32, (MOBA_BLOCK, tq), 1)
    own = [jnp.where(key_pos <= q_pos, scores(a, w_own[a], blk * MOBA_BLOCK, MOBA_BLOCK), NEG_INF) for a in heads]
    init = []
    for a in heads:
        m0 = jnp.max(own[a], axis=0, keepdims=True)
        init += [m0, _dot(values(a, blk * MOBA_BLOCK, MOBA_BLOCK), jnp.exp2(own[a] - m0).astype(BF16))]

    step = PAST_STEP * MOBA_BLOCK

    def make_body(first_block, group):
        def body(i, carry):
            carry = list(carry)
            starts = [(first_block + i * group + u * PAST_STEP) * MOBA_BLOCK for u in range(group // PAST_STEP)]
            tiles = [[scores(a, w_past[a], start, step) for start in starts] for a in heads]
            for u, start in enumerate(starts):
                for a in heads:
                    m, acc = carry[2 * a:2 * a + 2]
                    s = tiles[a][u]
                    m_new = jnp.maximum(m, jnp.max(s, axis=0, keepdims=True))
                    p = jnp.exp2(s - m_new).astype(BF16)
                    carry[2 * a:2 * a + 2] = [m_new, jnp.exp2(m - m_new) * acc + _dot(values(a, start, step), p)]
            return tuple(carry)
        return body

    n_full = blk // PAST_GROUP
    n_tail = (blk - n_full * PAST_GROUP + PAST_TAIL - 1) // PAST_TAIL
    res = lax.fori_loop(0, n_full, make_body(0, PAST_GROUP), tuple(init))
    res = lax.fori_loop(0, n_tail, make_body(n_full * PAST_GROUP, PAST_TAIL), res)
    for a in heads:
        acc = res[2 * a + 1]
        o_ref[0, head_rows[a], :] = acc[0:HEAD_DIM, :] / acc[HEAD_DIM:HEAD_DIM + 1, :]


def _moba_prompt(qt, kt, vt):
    b = qt.shape[0]
    rows = PROMPT_HEADS * HEAD_DIM
    kv_spec = pl.BlockSpec((1, rows, SEQ), lambda bi, h, qi: (bi, h, 0))
    q_spec = pl.BlockSpec((1, rows, MOBA_BLOCK), lambda bi, h, qi: (bi, h, qi))
    return pl.pallas_call(
        _moba_prompt_kernel,
        grid=(b, N_HEADS // PROMPT_HEADS, SEQ // MOBA_BLOCK),
        in_specs=[q_spec, kv_spec, kv_spec],
        out_specs=q_spec,
        out_shape=jax.ShapeDtypeStruct((b, ATT_WIDTH, SEQ), F32),
        scratch_shapes=[pltpu.VMEM((PROMPT_HEADS, SEQ, LANES), BF16),
                        pltpu.VMEM((PROMPT_HEADS, SEQ // MOBA_BLOCK, LANES), F32)],
        compiler_params=_params(3, 58),
        name="moba_prompt",
    )(qt, kt, vt)


def _moba_sample_kernel(pt_ref, q_ref, kn_ref, vn_ref, ck_hbm, cv_hbm, o_ref,
                        buf, sem, s_ref, p_ref):
    b = pl.program_id(0)
    n_seq = pl.num_programs(0)
    nb = N_PAST_BLOCKS
    n_rows = N_HEADS * DEC_SEQ
    scale = HEAD_DIM ** -0.5

    def page_copy(cache, seq, g, half):
        slot = lax.rem(g, KV_SLOTS)
        page = pt_ref[seq, lax.rem(g, nb) * PAGES_PER_BLOCK + half]
        return pltpu.make_async_copy(
            cache.at[page],
            buf.at[slot, :, pl.ds(half * PAGE_SIZE, PAGE_SIZE)],
            sem.at[slot, half])

    def start_chunk(seq, g):
        @pl.when(g < nb)
        def _():
            for half in range(PAGES_PER_BLOCK):
                page_copy(ck_hbm, seq, g, half).start()

        @pl.when(g >= nb)
        def _():
            for half in range(PAGES_PER_BLOCK):
                page_copy(cv_hbm, seq, g, half).start()

    def start_ahead(g):
        nxt = g + KV_SLOTS

        @pl.when(nxt < 2 * nb)
        def _():
            start_chunk(b, nxt)

        @pl.when((nxt >= 2 * nb) & (b + 1 < n_seq))
        def _():
            start_chunk(b + 1, nxt - 2 * nb)

    def wait_chunk(g):
        for half in range(PAGES_PER_BLOCK):
            page_copy(ck_hbm, b, g, half).wait()

    @pl.when(b == 0)
    def _():
        for g in range(KV_SLOTS):
            start_chunk(b, jnp.int32(g))

    q = q_ref[0]
    lane_head = lax.broadcasted_iota(jnp.int32, (DEC_SEQ, ATT_WIDTH), 1) // HEAD_DIM
    qbd = jnp.concatenate([jnp.where(lane_head == h, q, 0.0) for h in range(N_HEADS)], axis=0)
    q_hi = qbd.astype(BF16)
    q_lo = (qbd - q_hi.astype(F32)).astype(BF16)
    q_stack = jnp.concatenate([q_hi, q_lo], axis=0)

    def raw_scores(stacked):
        return stacked[0:n_rows] + stacked[n_rows:]

    gate_idx = lax.broadcasted_iota(jnp.int32, (n_rows, GATE_PAD), 1)

    def k_body(i, gate):
        chunks = [i * KV_UNROLL + u for u in range(KV_UNROLL)]
        for g in chunks:
            wait_chunk(g)
        for g in chunks:
            kb = buf[lax.rem(g, KV_SLOTS)].astype(BF16)
            s = raw_scores(_dot(q_stack, kb))
            s_ref[g] = s * scale
            gate = jnp.where(gate_idx == g, jnp.sum(s, axis=1, keepdims=True) * (1.0 / MOBA_BLOCK), gate)
        for g in chunks:
            start_ahead(g)
        return gate

    gate = lax.fori_loop(0, nb // KV_UNROLL, k_body, jnp.zeros((n_rows, GATE_PAD), F32))
    sel = _topk_mask(gate, gate_idx < nb, axis=1)

    pad = jnp.zeros((LANES - DEC_SEQ, ATT_WIDTH), F32)
    kn = jnp.concatenate([kn_ref[0], pad], axis=0).astype(BF16)
    vn = jnp.concatenate([vn_ref[0], pad], axis=0).astype(BF16)
    t_row = lax.rem(lax.broadcasted_iota(jnp.int32, (n_rows, LANES), 0), DEC_SEQ)
    t_col = lax.broadcasted_iota(jnp.int32, (n_rows, LANES), 1)
    s_own = jnp.where(t_col <= t_row, raw_scores(_dot_nt(q_stack, kn)) * scale, NEG_INF)

    m_lanes = jnp.full((n_rows, MOBA_BLOCK), NEG_INF, F32)
    for n in range(nb):
        m_lanes = jnp.maximum(m_lanes, jnp.where(sel[:, n:n + 1] > 0.5, s_ref[n], NEG_INF))
    m = jnp.maximum(jnp.max(s_own, axis=1, keepdims=True), jnp.max(m_lanes, axis=1, keepdims=True))
    p_own = jnp.exp(s_own - m)
    l_lanes = jnp.zeros((n_rows, MOBA_BLOCK), F32)
    for n in range(nb):
        pn = jnp.exp(jnp.where(sel[:, n:n + 1] > 0.5, s_ref[n], NEG_INF) - m)
        l_lanes = l_lanes + pn
        p_ref[n] = pn.astype(BF16)
    l = jnp.sum(p_own, axis=1, keepdims=True) + jnp.sum(l_lanes, axis=1, keepdims=True)

    def v_body(i, acc):
        chunks = [nb + i * KV_UNROLL + u for u in range(KV_UNROLL)]
        for g in chunks:
            wait_chunk(g)
        for g in chunks:
            vb = buf[lax.rem(g, KV_SLOTS)].astype(BF16)
            acc = acc + _dot_nt(p_ref[g - nb], vb)
        for g in chunks:
            start_ahead(g)
        return acc

    acc = lax.fori_loop(0, nb // KV_UNROLL, v_body, _dot(p_own.astype(BF16), vn))
    o = acc / l
    out = jnp.zeros((DEC_SEQ, ATT_WIDTH), F32)
    for h in range(N_HEADS):
        out = jnp.where(lane_head == h, o[h * DEC_SEQ:(h + 1) * DEC_SEQ, :], out)
    o_ref[0] = out


def _moba_sample(q, k_new, v_new, cache_kt, cache_vt, page_table):
    db = q.shape[0]
    n_rows = N_HEADS * DEC_SEQ
    tok_spec = pl.BlockSpec((1, DEC_SEQ, ATT_WIDTH), lambda b, pt: (b, 0, 0))
    grid_spec = pltpu.PrefetchScalarGridSpec(
        num_scalar_prefetch=1,
        grid=(db,),
        in_specs=[tok_spec, tok_spec, tok_spec,
                  pl.BlockSpec(memory_space=pl.ANY), pl.BlockSpec(memory_space=pl.ANY)],
        out_specs=tok_spec,
        scratch_shapes=[
            pltpu.VMEM((KV_SLOTS, ATT_WIDTH, MOBA_BLOCK), F32),
            pltpu.SemaphoreType.DMA((KV_SLOTS, PAGES_PER_BLOCK)),
            pltpu.VMEM((N_PAST_BLOCKS, n_rows, MOBA_BLOCK), F32),
            pltpu.VMEM((N_PAST_BLOCKS, n_rows, MOBA_BLOCK), BF16),
        ],
    )
    return pl.pallas_call(
        _moba_sample_kernel,
        grid_spec=grid_spec,
        out_shape=jax.ShapeDtypeStruct((db, DEC_SEQ, ATT_WIDTH), F32),
        compiler_params=_params(1, 40),
        name="moba_sample",
    )(page_table, q, k_new, v_new, cache_kt, cache_vt)


def _mix_out_rows(pool, att, w_ref, g, res):
    y = _dot(pool.astype(BF16), w_ref[0:POOL_WIDTH, :])
    y = y + _dot(att.astype(BF16), w_ref[POOL_WIDTH:, :])
    return res + _rmsnorm(y, g)


def _mix_out_kernel(a_ref, b_ref, w_ref, g_ref, r_ref, o_ref):
    o_ref[...] = _mix_out_rows(a_ref[...], b_ref[...], w_ref, g_ref[...], r_ref[...])


def _mix_out(pool, att, w_bf, g, res):
    m = res.shape[0]
    tm = min(ROW_TILE, m)
    half_spec = pl.BlockSpec((tm, POOL_WIDTH), lambda i: (i, 0))
    row_spec = pl.BlockSpec((tm, D_MODEL), lambda i: (i, 0))
    return pl.pallas_call(
        _mix_out_kernel,
        grid=(m // tm,),
        in_specs=[half_spec, half_spec,
                  pl.BlockSpec((D_MODEL, D_MODEL), lambda i: (0, 0)),
                  pl.BlockSpec((1, D_MODEL), lambda i: (0, 0)),
                  row_spec],
        out_specs=row_spec,
        out_shape=jax.ShapeDtypeStruct((m, D_MODEL), F32),
        compiler_params=_params(1, 32),
        name="mix_out",
    )(pool, att, w_bf, g, res)


def _mem_kv_kernel(x_ref, g_ref, wk_ref, wv_ref, k_ref, v_ref):
    n = _rmsnorm(x_ref[...], g_ref[...]).astype(BF16)
    k_ref[...] = _dot(n, wk_ref[...])
    v_ref[...] = _dot(n, wv_ref[...])


def _mem_kv(mem, g, wk_bf, wv_bf):
    m = mem.shape[0]
    tm = MEM_LEN
    row_spec = pl.BlockSpec((tm, D_MODEL), lambda i: (i, 0))
    w_spec = pl.BlockSpec((D_MODEL, D_MODEL), lambda i: (0, 0))
    out = jax.ShapeDtypeStruct((m, D_MODEL), F32)
    return pl.pallas_call(
        _mem_kv_kernel,
        grid=(m // tm,),
        in_specs=[row_spec, pl.BlockSpec((1, D_MODEL), lambda i: (0, 0)), w_spec, w_spec],
        out_specs=[row_spec, row_spec],
        out_shape=[out, out],
        compiler_params=_params(1, 32),
        name="mem_kv",
    )(mem, g, wk_bf, wv_bf)


def _xattn_rows(h, gpre, wq_ref, mk_ref, mv_ref, wo_ref, gpost):
    scale = MEM_HEAD_DIM ** -0.5
    q = _dot(_rmsnorm(h, gpre).astype(BF16), wq_ref[...]).astype(BF16)
    y = jnp.zeros(h.shape, F32)
    for hh in range(MEM_HEADS):
        cols = slice(hh * MEM_HEAD_DIM, (hh + 1) * MEM_HEAD_DIM)
        s = _dot_nt(q[:, cols], mk_ref[0, :, cols].astype(BF16)) * scale
        p = jnp.exp(s - jnp.max(s, axis=-1, keepdims=True))
        p = (p / jnp.sum(p, axis=-1, keepdims=True)).astype(BF16)
        oh = _dot(p, mv_ref[0, :, cols].astype(BF16))
        y = y + _dot(oh.astype(BF16), wo_ref[cols, :])
    return h + _rmsnorm(y, gpost)

def _xattn_sample_kernel(h_ref, gpre_ref, wq_ref, mk_ref, mv_ref, wo_ref, gpost_ref, o_ref):
    tm = h_ref.shape[0]
    n_mem = mk_ref.shape[0]
    t_rows = tm // n_mem
    width = MEM_LEN * MEM_ROWS
    n_q = MEM_HEADS * t_rows
    scale = MEM_HEAD_DIM ** -0.5
    h = h_ref[...]
    q = _dot(_rmsnorm(h, gpre_ref[...]).astype(BF16), wq_ref[...]).astype(BF16)
    col = lax.broadcasted_iota(jnp.int32, (n_q, width), 1)
    row_head = lax.broadcasted_iota(jnp.int32, (n_q, width), 0) // t_rows
    own = (lax.rem(col, MEM_ROWS) == row_head)
    rows = []
    for b in range(n_mem):
        keys = mk_ref[b].reshape(width, LANES).astype(BF16)
        vals = mv_ref[b].reshape(width, LANES).astype(BF16)
        qb = q[b * t_rows:(b + 1) * t_rows, :]
        q_all = jnp.concatenate([qb[:, hh * MEM_HEAD_DIM + c * LANES:hh * MEM_HEAD_DIM + (c + 1) * LANES]
                                 for c in range(MEM_CHUNKS) for hh in range(MEM_HEADS)], axis=0)
        s_all = _dot_nt(q_all, keys)
        s = s_all[0:n_q]
        for c in range(1, MEM_CHUNKS):
            s = s + pltpu.roll(s_all[c * n_q:(c + 1) * n_q], width - c * MEM_HEADS, 1)
        s = jnp.where(own, s * scale, NEG_INF)
        p = jnp.exp(s - jnp.max(s, axis=-1, keepdims=True))
        p = p / jnp.sum(p, axis=-1, keepdims=True)
        p_all = jnp.concatenate([p] + [pltpu.roll(p, c * MEM_HEADS, 1) for c in range(1, MEM_CHUNKS)], axis=0)
        o_all = _dot(p_all.astype(BF16), vals)
        rows.append(jnp.concatenate(
            [o_all[(c * MEM_HEADS + hh) * t_rows:(c * MEM_HEADS + hh + 1) * t_rows, :]
             for hh in range(MEM_HEADS) for c in range(MEM_CHUNKS)], axis=1))
    o = jnp.concatenate(rows, axis=0)
    o_ref[...] = h + _rmsnorm(_dot(o.astype(BF16), wo_ref[...]), gpost_ref[...])


def _xattn_sample(h, gpre, wq_bf, mk, mv, wo_bf, gpost, n_mem):
    m = h.shape[0]
    tm = n_mem * (m // mk.shape[0])
    row_spec = pl.BlockSpec((tm, D_MODEL), lambda i: (i, 0))
    vec_spec = pl.BlockSpec((1, D_MODEL), lambda i: (0, 0))
    w_spec = pl.BlockSpec((D_MODEL, D_MODEL), lambda i: (0, 0))
    mem_spec = pl.BlockSpec((n_mem, MEM_LEN, MEM_ROWS, LANES), lambda i: (i, 0, 0, 0))
    return pl.pallas_call(
        _xattn_sample_kernel,
        grid=(m // tm,),
        in_specs=[row_spec, vec_spec, w_spec, mem_spec, mem_spec, w_spec, vec_spec],
        out_specs=row_spec,
        out_shape=jax.ShapeDtypeStruct((m, D_MODEL), F32),
        compiler_params=_params(1, 48),
        name="xattn_sample",
    )(h, gpre, wq_bf, mk, mv, wo_bf, gpost)


def _ffn_rows(h, gpre, wu_ref, wd_ref, gpost):
    n = _rmsnorm(h, gpre).astype(BF16)
    y = jnp.zeros(h.shape, F32)
    chunk = D_MODEL
    for c in range(D_FF // chunk):
        a = jnp.square(jnp.maximum(_dot(n, wu_ref[:, c * chunk:(c + 1) * chunk]), 0.0))
        y = y + _dot(a.astype(BF16), wd_ref[c * chunk:(c + 1) * chunk, :])
    return h + _rmsnorm(y, gpost)


def _ffn_kernel(h_ref, gpre_ref, wu_ref, wd_ref, gpost_ref, o_ref):
    o_ref[...] = _ffn_rows(h_ref[...], gpre_ref[...], wu_ref, wd_ref, gpost_ref[...])


def _post_prompt_kernel(pool_ref, att_ref, x_ref, wout_ref, gmix_ref, gprex_ref, wq_ref, mk_ref, mv_ref, wo_ref,
                        gpostx_ref, gpreffn_ref, wu_ref, wd_ref, gpostffn_ref, o_ref):
    h = _mix_out_rows(pool_ref[...], att_ref[0].T, wout_ref, gmix_ref[...], x_ref[...])
    h = _xattn_rows(h, gprex_ref[...], wq_ref, mk_ref, mv_ref, wo_ref, gpostx_ref[...])
    o_ref[...] = _ffn_rows(h, gpreffn_ref[...], wu_ref, wd_ref, gpostffn_ref[...])


def _post_prompt(pool, att_t, x, w_out_bf, g_mix, g_pre_x, wq_bf, mk, mv, wo_bf, g_post_x,
                 g_pre_ffn, wu_bf, wd_bf, g_post_ffn):
    m = x.shape[0]
    tm = ROW_TILE
    tiles = att_t.shape[2] // tm

    def resident(shape):
        return pl.BlockSpec(shape, lambda i: (0,) * len(shape), pipeline_mode=pl.Buffered(1))

    row_spec = pl.BlockSpec((tm, D_MODEL), lambda i: (i, 0))
    vec_spec = resident((1, D_MODEL))
    sq_spec = resident((D_MODEL, D_MODEL))
    mem_spec = pl.BlockSpec((1, MEM_LEN, D_MODEL), lambda i: (i // tiles, 0, 0))
    return pl.pallas_call(
        _post_prompt_kernel,
        grid=(m // tm,),
        in_specs=[pl.BlockSpec((tm, POOL_WIDTH), lambda i: (i, 0)),
                  pl.BlockSpec((1, ATT_WIDTH, tm), lambda i: (i // tiles, 0, i % tiles)),
                  row_spec, sq_spec, vec_spec, vec_spec, sq_spec, mem_spec, mem_spec, sq_spec, vec_spec,
                  vec_spec, resident((D_MODEL, D_FF)), resident((D_FF, D_MODEL)), vec_spec],
        out_specs=row_spec,
        out_shape=jax.ShapeDtypeStruct((m, D_MODEL), F32),
        compiler_params=_params(1, 58),
        name="post_prompt",
    )(pool, att_t, x, w_out_bf, g_mix, g_pre_x, wq_bf, mk, mv, wo_bf, g_post_x, g_pre_ffn, wu_bf, wd_bf, g_post_ffn)


def _ffn(h, gpre, wu_bf, wd_bf, gpost):
    m = h.shape[0]
    tm = min(ROW_TILE, m)
    row_spec = pl.BlockSpec((tm, D_MODEL), lambda i: (i, 0))
    vec_spec = pl.BlockSpec((1, D_MODEL), lambda i: (0, 0))
    return pl.pallas_call(
        _ffn_kernel,
        grid=(m // tm,),
        in_specs=[row_spec, vec_spec,
                  pl.BlockSpec((D_MODEL, D_FF), lambda i: (0, 0)),
                  pl.BlockSpec((D_FF, D_MODEL), lambda i: (0, 0)),
                  vec_spec],
        out_specs=row_spec,
        out_shape=jax.ShapeDtypeStruct((m, D_MODEL), F32),
        compiler_params=_params(1, 56),
        name="ffn",
    )(h, gpre, wu_bf, wd_bf, gpost)


def _rope_inv():
    inv = ROPE_THETA ** (-2.0 * jnp.arange(ROT_HALF, dtype=F32) / ROT_DIM)
    d = np.arange(LANES) % HEAD_DIM
    rotary = (d < ROT_DIM).astype(np.float32)
    inv_lanes = (inv[d % ROT_HALF] * rotary).reshape(1, LANES)
    return inv_lanes, inv.reshape(ROT_HALF, 1)


def kernel(x_prompt, x_sample, mem_prompt, cache_k, cache_v, cache_mem_k, cache_mem_v, state_pool, page_table,
           g_pre_mix, w_in, w_pool, pool_scale, w_out, g_post_mix,
           g_mem, g_pre_x, w_xq, w_xk, w_xv, w_xo, g_post_x,
           g_pre_ffn, w_up, w_down, g_post_ffn):
    depth = w_in.shape[0]
    batch = x_prompt.shape[0]
    db = x_sample.shape[0]
    n_phys = cache_k.shape[1]
    inv_lanes, inv_col = _rope_inv()

    hp = x_prompt.reshape(batch * SEQ, D_MODEL)
    hs = x_sample.reshape(db * DEC_SEQ, D_MODEL)
    pool_p, k_p, v_p, mk_p, mv_p, pool_s, k_s, v_s = ([] for _ in range(8))
    for l in range(depth):
        vec = lambda a: a[l].reshape(1, -1)
        w_in_bf = w_in[l].astype(BF16)
        w_u_bf = w_in_bf[:, :POOL_WIDTH]
        w_qkvt_bf = w_in_bf[:, POOL_WIDTH:].T
        w_pool_bf = w_pool[l].astype(BF16)
        w_out_bf = w_out[l].astype(BF16)
        w_xq_bf, w_xo_bf = w_xq[l].astype(BF16), w_xo[l].astype(BF16)
        w_up_bf, w_down_bf = w_up[l].astype(BF16), w_down[l].astype(BF16)

        pool, u_tail, qt, kt, vt = _in_proj_prompt(hp, vec(g_pre_mix), w_u_bf, w_qkvt_bf, inv_col,
                                                   w_pool_bf, vec(pool_scale), batch)
        mk, mv = _mem_kv(mem_prompt.reshape(batch * MEM_LEN, D_MODEL), vec(g_mem),
                         w_xk[l].astype(BF16), w_xv[l].astype(BF16))
        hp = _post_prompt(pool, _moba_prompt(qt, kt, vt), hp, w_out_bf, vec(g_post_mix),
                          vec(g_pre_x), w_xq_bf, mk.reshape(batch, MEM_LEN, D_MODEL),
                          mv.reshape(batch, MEM_LEN, D_MODEL), w_xo_bf, vec(g_post_x),
                          vec(g_pre_ffn), w_up_bf, w_down_bf, vec(g_post_ffn))
        pool_p.append(u_tail[:, POOL_HALO - POOL_STATE:])
        k_p.append(kt.reshape(batch, N_HEADS, HEAD_DIM, SEQ).transpose(0, 3, 1, 2))
        v_p.append(vt.reshape(batch, N_HEADS, HEAD_DIM, SEQ).transpose(0, 3, 1, 2))
        mk_p.append(mk.reshape(batch, MEM_LEN, MEM_HEADS, MEM_HEAD_DIM))
        mv_p.append(mv.reshape(batch, MEM_LEN, MEM_HEADS, MEM_HEAD_DIM))

        u, q, k, v = _in_proj_rows(hs, vec(g_pre_mix), w_in_bf, inv_lanes, DEC_SEQ, PAST_LEN)
        u_ext = jnp.concatenate([state_pool[l], u.reshape(db, DEC_SEQ, POOL_WIDTH)], axis=1)
        pool = _pool_sample(jnp.pad(u_ext, ((0, 0), (1, 0), (0, 0))), w_pool_bf, vec(pool_scale))
        cache_kt = cache_k[l].transpose(0, 2, 3, 1).reshape(n_phys, ATT_WIDTH, PAGE_SIZE)
        cache_vt = cache_v[l].transpose(0, 2, 3, 1).reshape(n_phys, ATT_WIDTH, PAGE_SIZE)
        att = _moba_sample(q.reshape(db, DEC_SEQ, ATT_WIDTH), k.reshape(db, DEC_SEQ, ATT_WIDTH),
                           v.reshape(db, DEC_SEQ, ATT_WIDTH), cache_kt, cache_vt, page_table)
        hs = _mix_out(pool, att.reshape(db * DEC_SEQ, ATT_WIDTH), w_out_bf, vec(g_post_mix), hs)
        mem_native = lambda a: a[l].reshape(db, MEM_LEN, MEM_HEADS, MEM_CHUNKS, LANES).transpose(
            0, 1, 3, 2, 4).reshape(db, MEM_LEN, MEM_ROWS, LANES)
        hs = _xattn_sample(hs, vec(g_pre_x), w_xq_bf, mem_native(cache_mem_k), mem_native(cache_mem_v),
                           w_xo_bf, vec(g_post_x), n_mem=4)
        hs = _ffn(hs, vec(g_pre_ffn), w_up_bf, w_down_bf, vec(g_post_ffn))
        pool_s.append(u_ext[:, u_ext.shape[1] - POOL_STATE:])
        k_s.append(k.reshape(db, DEC_SEQ, N_HEADS, HEAD_DIM))
        v_s.append(v.reshape(db, DEC_SEQ, N_HEADS, HEAD_DIM))

    return (hp.reshape(batch, SEQ, D_MODEL), hs.reshape(db, DEC_SEQ, D_MODEL),
            jnp.stack(pool_p), jnp.stack(k_p), jnp.stack(v_p), jnp.stack(mk_p), jnp.stack(mv_p),
            jnp.stack(pool_s), jnp.stack(k_s), jnp.stack(v_s))
```

```python
import functools

import numpy as np
import jax
import jax.numpy as jnp
from jax import lax
from jax.experimental import pallas as pl
from jax.experimental.pallas import tpu as pltpu

F32 = jnp.float32
BF16 = jnp.bfloat16

D_MODEL = 1024
SEQ = 8192
DEC_SEQ = 8
PAST_LEN = 8192
PAGE_SIZE = 128
POOL_WIDTH = 512
POOL_WINDOWS = (2, 4, 8, 16)
POOL_GW = 128
POOL_STATE = 15
ATT_WIDTH = 512
N_HEADS = 8
HEAD_DIM = 64
ROT_DIM = 16
ROPE_THETA = 500000.0
MOBA_BLOCK = 256
MOBA_TOPK = 3
MEM_LEN = 256
MEM_HEADS = 4
MEM_HEAD_DIM = 256
D_FF = 4096
EPS = 1e-6

LANES = 128
SUBLANES = 8

ROW_TILE = 512
POOL_HALO = 16
N_PAST_BLOCKS = PAST_LEN // MOBA_BLOCK
PAGES_PER_BLOCK = MOBA_BLOCK // PAGE_SIZE
GATE_PAD = LANES
KV_SLOTS = 32
KV_UNROLL = 4
PAST_GROUP = 8
PAST_STEP = 1
PAST_TAIL = 4
PROMPT_HEADS = 4
ONES_ROWS = 16
ROT_HALF = ROT_DIM // 2
MEM_CHUNKS = MEM_HEAD_DIM // LANES
MEM_ROWS = MEM_CHUNKS * MEM_HEADS
NEG_INF = float("-inf")
MASK_BIAS = -1e30
LOG2_E = 1.4426950408889634

assert PAST_LEN % MOBA_BLOCK == 0, "sample own-block is assumed to hold new keys only"
assert PAST_LEN + 1 >= max(POOL_WINDOWS)
assert ROT_HALF == SUBLANES, "feature-major rotary assumes one sublane group per rotary half"
assert (2 * N_PAST_BLOCKS) % KV_SLOTS == 0 and N_PAST_BLOCKS % KV_UNROLL == 0 and KV_SLOTS % KV_UNROLL == 0
assert (SEQ // MOBA_BLOCK) % PAST_GROUP == 0 and HEAD_DIM + SEQ // MOBA_BLOCK <= LANES
assert PAST_GROUP % PAST_TAIL == 0 and PAST_TAIL % PAST_STEP == 0


def _params(n_grid, vmem_mib):
    return pltpu.CompilerParams(
        dimension_semantics=("arbitrary",) * n_grid,
        vmem_limit_bytes=vmem_mib * 1024 * 1024,
    )


def _rmsnorm(x, g):
    ms = jnp.mean(x * x, axis=-1, keepdims=True)
    return x * lax.rsqrt(ms + EPS) * g


def _dot(a, b):
    return jnp.dot(a, b, preferred_element_type=F32)


def _dot_nt(a, b, precision=None):
    return lax.dot_general(a, b, (((1,), (1,)), ((), ())), precision=precision,
                           preferred_element_type=F32)


def _topk_mask(gate, valid, axis):
    n = gate.shape[axis]
    idx = lax.broadcasted_iota(jnp.int32, gate.shape, axis).astype(F32)
    g = jnp.where(valid, gate, NEG_INF)
    sel = jnp.zeros(gate.shape, F32)
    for _ in range(MOBA_TOPK):
        mx = jnp.max(g, axis=axis, keepdims=True)
        first = jnp.min(jnp.where(g == mx, idx, float(n)), axis=axis, keepdims=True)
        pick = idx == first
        sel = jnp.where(pick, 1.0, sel)
        g = jnp.where(pick, NEG_INF, g)
    return jnp.where(valid, sel, 0.0)


def _rope_rows(z, pos_rows, inv_lanes):
    tm = z.shape[0]
    ang = pos_rows * inv_lanes
    cos = jnp.cos(ang)
    sin = jnp.sin(ang)
    d = lax.rem(lax.broadcasted_iota(jnp.int32, (tm, LANES), 1), HEAD_DIM)
    sin_lo = jnp.where(d < ROT_HALF, -sin, 0.0)
    sin_hi = jnp.where((d >= ROT_HALF) & (d < ROT_DIM), sin, 0.0)
    out = []
    for c in range(ATT_WIDTH // LANES):
        zz = z[:, c * LANES:(c + 1) * LANES]
        out.append(zz * cos
                   + pltpu.roll(zz, LANES - ROT_HALF, 1) * sin_lo
                   + pltpu.roll(zz, ROT_HALF, 1) * sin_hi)
    return out


def _in_proj_rows_kernel(x_ref, g_ref, w_ref, inv_ref, u_ref, q_ref, k_ref, v_ref, *, period, offset):
    tm = x_ref.shape[0]
    i = pl.program_id(0)
    n = _rmsnorm(x_ref[...], g_ref[...]).astype(BF16)

    def seg(j):
        return _dot(n, w_ref[:, j * ATT_WIDTH:(j + 1) * ATT_WIDTH])

    u_ref[...] = seg(0)
    v_ref[...] = seg(3)
    row = lax.broadcasted_iota(jnp.int32, (tm, LANES), 0) + i * tm
    pos = (lax.rem(row, period) + offset).astype(F32)
    for j, ref in ((1, q_ref), (2, k_ref)):
        for c, piece in enumerate(_rope_rows(seg(j), pos, inv_ref[...])):
            ref[:, c * LANES:(c + 1) * LANES] = piece


def _in_proj_rows(x, g, w_bf, inv_lanes, period, offset):
    m = x.shape[0]
    tm = min(ROW_TILE, m)
    out = jax.ShapeDtypeStruct((m, ATT_WIDTH), F32)
    row_spec = pl.BlockSpec((tm, ATT_WIDTH), lambda i: (i, 0))
    return pl.pallas_call(
        functools.partial(_in_proj_rows_kernel, period=period, offset=offset),
        grid=(m // tm,),
        in_specs=[
            pl.BlockSpec((tm, D_MODEL), lambda i: (i, 0)),
            pl.BlockSpec((1, D_MODEL), lambda i: (0, 0)),
            pl.BlockSpec((D_MODEL, 4 * ATT_WIDTH), lambda i: (0, 0)),
            pl.BlockSpec((1, LANES), lambda i: (0, 0)),
        ],
        out_specs=[row_spec] * 4,
        out_shape=[out] * 4,
        compiler_params=_params(1, 40),
        name="in_proj_rows",
    )(x, g, w_bf, inv_lanes)


def _in_proj_prompt_kernel(x_ref, g_ref, wu_ref, wqkvt_ref, invc_ref, wpool_ref, sc_ref,
                           pool_ref, tail_ref, qt_ref, kt_ref, vt_ref, ext_ref):
    tm = x_ref.shape[0]
    i = pl.program_id(1)
    n = _rmsnorm(x_ref[...], g_ref[...]).astype(BF16)
    u = _dot(n, wu_ref[...])

    @pl.when(i == 0)
    def _():
        ext_ref[0:POOL_HALO, :] = jnp.zeros((POOL_HALO, POOL_WIDTH), F32)

    @pl.when(i > 0)
    def _():
        ext_ref[0:POOL_HALO, :] = ext_ref[tm:tm + POOL_HALO, :]

    ext_ref[POOL_HALO:, :] = u
    tail_ref[0] = u[tm - POOL_HALO:, :]
    _pool_mix_rows(u, ext_ref, i * tm, wpool_ref, sc_ref, pool_ref)

    vt_ref[0] = _dot_nt(wqkvt_ref[2 * ATT_WIDTH:, :], n)
    pos_t = (lax.broadcasted_iota(jnp.int32, (ROT_HALF, tm), 1) + i * tm).astype(F32)
    ang = invc_ref[...] * pos_t
    cos = jnp.cos(ang)
    sin = jnp.sin(ang)
    for off, ref in ((0, qt_ref), (ATT_WIDTH, kt_ref)):
        zt = _dot_nt(wqkvt_ref[off:off + ATT_WIDTH, :], n)
        for h in range(N_HEADS):
            r0 = h * HEAD_DIM
            x1 = zt[r0:r0 + ROT_HALF, :]
            x2 = zt[r0 + ROT_HALF:r0 + ROT_DIM, :]
            ref[0, r0:r0 + ROT_HALF, :] = x1 * cos - x2 * sin
            ref[0, r0 + ROT_HALF:r0 + ROT_DIM, :] = x2 * cos + x1 * sin
            ref[0, r0 + ROT_DIM:r0 + HEAD_DIM, :] = zt[r0 + ROT_DIM:r0 + HEAD_DIM, :]


def _in_proj_prompt(x, g, wu_bf, wqkvt_bf, inv_col, w_pool_bf, pool_scale, batch):
    tm = ROW_TILE
    tiles = SEQ // tm
    t_spec = pl.BlockSpec((1, ATT_WIDTH, tm), lambda b, i: (b, 0, i))
    feat = jax.ShapeDtypeStruct((batch, ATT_WIDTH, SEQ), F32)
    return pl.pallas_call(
        _in_proj_prompt_kernel,
        grid=(batch, tiles),
        in_specs=[
            pl.BlockSpec((tm, D_MODEL), lambda b, i: (b * tiles + i, 0)),
            pl.BlockSpec((1, D_MODEL), lambda b, i: (0, 0)),
            pl.BlockSpec((D_MODEL, POOL_WIDTH), lambda b, i: (0, 0)),
            pl.BlockSpec((3 * ATT_WIDTH, D_MODEL), lambda b, i: (0, 0)),
            pl.BlockSpec((ROT_HALF, 1), lambda b, i: (0, 0)),
            pl.BlockSpec((len(POOL_WINDOWS), POOL_GW, POOL_GW), lambda b, i: (0, 0, 0)),
            pl.BlockSpec((1, POOL_WIDTH), lambda b, i: (0, 0)),
        ],
        out_specs=[pl.BlockSpec((tm, POOL_WIDTH), lambda b, i: (b * tiles + i, 0)),
                   pl.BlockSpec((1, POOL_HALO, POOL_WIDTH), lambda b, i: (b, 0, 0)),
                   t_spec, t_spec, t_spec],
        out_shape=[jax.ShapeDtypeStruct((batch * SEQ, POOL_WIDTH), F32),
                   jax.ShapeDtypeStruct((batch, POOL_HALO, POOL_WIDTH), F32), feat, feat, feat],
        scratch_shapes=[pltpu.VMEM((POOL_HALO + tm, POOL_WIDTH), F32)],
        compiler_params=_params(2, 40),
        name="in_proj_prompt",
    )(x, g, wu_bf, wqkvt_bf, inv_col, w_pool_bf, pool_scale)


def _pool_mix_rows(u, ext_ref, pos0, w_ref, sc_ref, o_ref):
    tm = u.shape[0]
    pos = pos0 + lax.broadcasted_iota(jnp.int32, (tm, POOL_GW), 0)
    for g, w in enumerate(POOL_WINDOWS):
        cols = slice(g * POOL_GW, (g + 1) * POOL_GW)
        cur = u[:, cols]
        s = cur
        for k in range(1, w):
            s = s + ext_ref[POOL_HALO - k:POOL_HALO - k + tm, cols]
        cnt = jnp.minimum(pos + 1, w).astype(F32)
        d = s / cnt - cur
        o_ref[:, cols] = _dot(d.astype(BF16), w_ref[g]) * sc_ref[:, cols]


def _pool_sample_kernel(e_ref, w_ref, sc_ref, o_ref):
    nb = e_ref.shape[0]
    lo = 1 + POOL_STATE
    t = lax.broadcasted_iota(jnp.int32, (nb, DEC_SEQ, POOL_GW), 1)
    for g, w in enumerate(POOL_WINDOWS):
        cols = slice(g * POOL_GW, (g + 1) * POOL_GW)
        cur = e_ref[:, lo:lo + DEC_SEQ, cols]
        s = cur
        for k in range(1, w):
            s = s + e_ref[:, lo - k:lo - k + DEC_SEQ, cols]
        cnt = jnp.minimum(PAST_LEN + t + 1, w).astype(F32)
        d = (s / cnt - cur).reshape(nb * DEC_SEQ, POOL_GW)
        o_ref[:, cols] = _dot(d.astype(BF16), w_ref[g]) * sc_ref[:, cols]


def _pool_sample(ext, w_pool_bf, pool_scale):
    db, rows, _ = ext.shape
    nb = 32
    return pl.pallas_call(
        _pool_sample_kernel,
        grid=(db // nb,),
        in_specs=[
            pl.BlockSpec((nb, rows, POOL_WIDTH), lambda i: (i, 0, 0)),
            pl.BlockSpec((len(POOL_WINDOWS), POOL_GW, POOL_GW), lambda i: (0, 0, 0)),
            pl.BlockSpec((1, POOL_WIDTH), lambda i: (0, 0)),
        ],
        out_specs=pl.BlockSpec((nb * DEC_SEQ, POOL_WIDTH), lambda i: (i, 0)),
        out_shape=jax.ShapeDtypeStruct((db * DEC_SEQ, POOL_WIDTH), F32),
        compiler_params=_params(1, 32),
        name="pool_sample",
    )(ext, w_pool_bf, pool_scale)


def _moba_prompt_kernel(qt_ref, kt_ref, vt_ref, o_ref, kaug_ref, kmean_ref):
    tq = MOBA_BLOCK
    blk = pl.program_id(2)
    n_blocks = SEQ // MOBA_BLOCK
    pad_rows = LANES - HEAD_DIM - n_blocks

    heads = range(PROMPT_HEADS)
    head_rows = [slice(a * HEAD_DIM, (a + 1) * HEAD_DIM) for a in heads]

    @pl.when(blk == 0)
    def _():
        block_row = lax.broadcasted_iota(jnp.int32, (n_blocks, MOBA_BLOCK), 0)
        lane = lax.broadcasted_iota(jnp.int32, (1, LANES), 1)
        for a in heads:
            for n in range(n_blocks):
                kb = kt_ref[0, head_rows[a], n * MOBA_BLOCK:(n + 1) * MOBA_BLOCK]
                aug = jnp.concatenate([kb, jnp.where(block_row == n, 1.0, 0.0),
                                       jnp.zeros((pad_rows, MOBA_BLOCK), F32)], axis=0).T
                kaug_ref[a, n * MOBA_BLOCK:(n + 1) * MOBA_BLOCK, :] = aug.astype(BF16)
                kmean_ref[a, n:n + 1, :] = jnp.where(
                    lane < HEAD_DIM, jnp.sum(aug, axis=0, keepdims=True) * (1.0 / MOBA_BLOCK), 0.0)

    n_idx = lax.broadcasted_iota(jnp.int32, (n_blocks, tq), 0)
    w_past, w_own = [], []
    for a in heads:
        qt = qt_ref[0, head_rows[a], :]
        gate = jnp.dot(kmean_ref[a], jnp.concatenate([qt, jnp.zeros((LANES - HEAD_DIM, tq), F32)], axis=0),
                       precision=lax.Precision.HIGHEST, preferred_element_type=F32)
        sel = _topk_mask(gate, n_idx < blk, axis=0)
        bias = jnp.where(sel > 0.5, 0.0, MASK_BIAS)
        q_log2 = qt * (HEAD_DIM ** -0.5 * LOG2_E)
        w_past.append(jnp.concatenate([q_log2, bias, jnp.zeros((pad_rows, tq), F32)], axis=0).astype(BF16))
        w_own.append(jnp.concatenate([q_log2, jnp.zeros((LANES - HEAD_DIM, tq), F32)], axis=0).astype(BF16))

    def scores(a, w, start, size):
        return _dot(kaug_ref[a, pl.ds(pl.multiple_of(start, MOBA_BLOCK), size), :], w)

    def values(a, start, size):
        vb = vt_ref[0, head_rows[a], pl.ds(pl.multiple_of(start, MOBA_BLOCK), size)]
        return jnp.concatenate([vb.astype(BF16), jnp.ones((ONES_ROWS, size), BF16)], axis=0)

    key_pos = lax.broadcasted_iota(jnp.int32, (MOBA_BLOCK, tq), 0)
    q_pos = lax.broadcasted_iota(jnp.int32, (MOBA_BLOCK, tq), 1)
    own = [jnp.where(key_pos <= q_pos, scores(a, w_own[a], blk * MOBA_BLOCK, MOBA_BLOCK), NEG_INF) for a in heads]
    init = []
    for a in heads:
        m0 = jnp.max(own[a], axis=0, keepdims=True)
        init += [m0, _dot(values(a, blk * MOBA_BLOCK, MOBA_BLOCK), jnp.exp2(own[a] - m0).astype(BF16))]

    step = PAST_STEP * MOBA_BLOCK

    def make_body(first_block, group):
        def body(i, carry):
            carry = list(carry)
            starts = [(first_block + i * group + u * PAST_STEP) * MOBA_BLOCK for u in range(group // PAST_STEP)]
            tiles = [[scores(a, w_past[a], start, step) for start in starts] for a in heads]
            for u, start in enumerate(starts):
                for a in heads:
                    m, acc = carry[2 * a:2 * a + 2]
                    s = tiles[a][u]
                    m_new = jnp.maximum(m, jnp.max(s, axis=0, keepdims=True))
                    p = jnp.exp2(s - m_new).astype(BF16)
                    carry[2 * a:2 * a + 2] = [m_new, jnp.exp2(m - m_new) * acc + _dot(values(a, start, step), p)]
            return tuple(carry)
        return body

    n_full = blk // PAST_GROUP
    n_tail = (blk - n_full * PAST_GROUP + PAST_TAIL - 1) // PAST_TAIL
    res = lax.fori_loop(0, n_full, make_body(0, PAST_GROUP), tuple(init))
    res = lax.fori_loop(0, n_tail, make_body(n_full * PAST_GROUP, PAST_TAIL), res)
    for a in heads:
        acc = res[2 * a + 1]
        o_ref[0, head_rows[a], :] = acc[0:HEAD_DIM, :] / acc[HEAD_DIM:HEAD_DIM + 1, :]


def _moba_prompt(qt, kt, vt):
    b = qt.shape[0]
    rows = PROMPT_HEADS * HEAD_DIM
    kv_spec = pl.BlockSpec((1, rows, SEQ), lambda bi, h, qi: (bi, h, 0))
    q_spec = pl.BlockSpec((1, rows, MOBA_BLOCK), lambda bi, h, qi: (bi, h, qi))
    return pl.pallas_call(
        _moba_prompt_kernel,
        grid=(b, N_HEADS // PROMPT_HEADS, SEQ // MOBA_BLOCK),
        in_specs=[q_spec, kv_spec, kv_spec],
        out_specs=q_spec,
        out_shape=jax.ShapeDtypeStruct((b, ATT_WIDTH, SEQ), F32),
        scratch_shapes=[pltpu.VMEM((PROMPT_HEADS, SEQ, LANES), BF16),
                        pltpu.VMEM((PROMPT_HEADS, SEQ // MOBA_BLOCK, LANES), F32)],
        compiler_params=_params(3, 58),
        name="moba_prompt",
    )(qt, kt, vt)


def _moba_sample_kernel(pt_ref, q_ref, kn_ref, vn_ref, ck_hbm, cv_hbm, o_ref,
                        buf, sem, s_ref, p_ref):
    b = pl.program_id(0)
    n_seq = pl.num_programs(0)
    nb = N_PAST_BLOCKS
    n_rows = N_HEADS * DEC_SEQ
    scale = HEAD_DIM ** -0.5

    def page_copy(cache, seq, g, half):
        slot = lax.rem(g, KV_SLOTS)
        page = pt_ref[seq, lax.rem(g, nb) * PAGES_PER_BLOCK + half]
        return pltpu.make_async_copy(
            cache.at[page],
            buf.at[slot, :, pl.ds(half * PAGE_SIZE, PAGE_SIZE)],
            sem.at[slot, half])

    def start_chunk(seq, g):
        @pl.when(g < nb)
        def _():
            for half in range(PAGES_PER_BLOCK):
                page_copy(ck_hbm, seq, g, half).start()

        @pl.when(g >= nb)
        def _():
            for half in range(PAGES_PER_BLOCK):
                page_copy(cv_hbm, seq, g, half).start()

    def start_ahead(g):
        nxt = g + KV_SLOTS

        @pl.when(nxt < 2 * nb)
        def _():
            start_chunk(b, nxt)

        @pl.when((nxt >= 2 * nb) & (b + 1 < n_seq))
        def _():
            start_chunk(b + 1, nxt - 2 * nb)

    def wait_chunk(g):
        for half in range(PAGES_PER_BLOCK):
            page_copy(ck_hbm, b, g, half).wait()

    @pl.when(b == 0)
    def _():
        for g in range(KV_SLOTS):
            start_chunk(b, jnp.int32(g))

    q = q_ref[0]
    lane_head = lax.broadcasted_iota(jnp.int32, (DEC_SEQ, ATT_WIDTH), 1) // HEAD_DIM
    qbd = jnp.concatenate([jnp.where(lane_head == h, q, 0.0) for h in range(N_HEADS)], axis=0)
    q_hi = qbd.astype(BF16)
    q_lo = (qbd - q_hi.astype(F32)).astype(BF16)
    q_stack = jnp.concatenate([q_hi, q_lo], axis=0)

    def raw_scores(stacked):
        return stacked[0:n_rows] + stacked[n_rows:]

    gate_idx = lax.broadcasted_iota(jnp.int32, (n_rows, GATE_PAD), 1)

    def k_body(i, gate):
        chunks = [i * KV_UNROLL + u for u in range(KV_UNROLL)]
        for g in chunks:
            wait_chunk(g)
        for g in chunks:
            kb = buf[lax.rem(g, KV_SLOTS)].astype(BF16)
            s = raw_scores(_dot(q_stack, kb))
            s_ref[g] = s * scale
            gate = jnp.where(gate_idx == g, jnp.sum(s, axis=1, keepdims=True) * (1.0 / MOBA_BLOCK), gate)
        for g in chunks:
            start_ahead(g)
        return gate

    gate = lax.fori_loop(0, nb // KV_UNROLL, k_body, jnp.zeros((n_rows, GATE_PAD), F32))
    sel = _topk_mask(gate, gate_idx < nb, axis=1)

    pad = jnp.zeros((LANES - DEC_SEQ, ATT_WIDTH), F32)
    kn = jnp.concatenate([kn_ref[0], pad], axis=0).astype(BF16)
    vn = jnp.concatenate([vn_ref[0], pad], axis=0).astype(BF16)
    t_row = lax.rem(lax.broadcasted_iota(jnp.int32, (n_rows, LANES), 0), DEC_SEQ)
    t_col = lax.broadcasted_iota(jnp.int32, (n_rows, LANES), 1)
    s_own = jnp.where(t_col <= t_row, raw_scores(_dot_nt(q_stack, kn)) * scale, NEG_INF)

    m_lanes = jnp.full((n_rows, MOBA_BLOCK), NEG_INF, F32)
    for n in range(nb):
        m_lanes = jnp.maximum(m_lanes, jnp.where(sel[:, n:n + 1] > 0.5, s_ref[n], NEG_INF))
    m = jnp.maximum(jnp.max(s_own, axis=1, keepdims=True), jnp.max(m_lanes, axis=1, keepdims=True))
    p_own = jnp.exp(s_own - m)
    l_lanes = jnp.zeros((n_rows, MOBA_BLOCK), F32)
    for n in range(nb):
        pn = jnp.exp(jnp.where(sel[:, n:n + 1] > 0.5, s_ref[n], NEG_INF) - m)
        l_lanes = l_lanes + pn
        p_ref[n] = pn.astype(BF16)
    l = jnp.sum(p_own, axis=1, keepdims=True) + jnp.sum(l_lanes, axis=1, keepdims=True)

    def v_body(i, acc):
        chunks = [nb + i * KV_UNROLL + u for u in range(KV_UNROLL)]
        for g in chunks:
            wait_chunk(g)
        for g in chunks:
            vb = buf[lax.rem(g, KV_SLOTS)].astype(BF16)
            acc = acc + _dot_nt(p_ref[g - nb], vb)
        for g in chunks:
            start_ahead(g)
        return acc

    acc = lax.fori_loop(0, nb // KV_UNROLL, v_body, _dot(p_own.astype(BF16), vn))
    o = acc / l
    out = jnp.zeros((DEC_SEQ, ATT_WIDTH), F32)
    for h in range(N_HEADS):
        out = jnp.where(lane_head == h, o[h * DEC_SEQ:(h + 1) * DEC_SEQ, :], out)
    o_ref[0] = out


def _moba_sample(q, k_new, v_new, cache_kt, cache_vt, page_table):
    db = q.shape[0]
    n_rows = N_HEADS * DEC_SEQ
    tok_spec = pl.BlockSpec((1, DEC_SEQ, ATT_WIDTH), lambda b, pt: (b, 0, 0))
    grid_spec = pltpu.PrefetchScalarGridSpec(
        num_scalar_prefetch=1,
        grid=(db,),
        in_specs=[tok_spec, tok_spec, tok_spec,
                  pl.BlockSpec(memory_space=pl.ANY), pl.BlockSpec(memory_space=pl.ANY)],
        out_specs=tok_spec,
        scratch_shapes=[
            pltpu.VMEM((KV_SLOTS, ATT_WIDTH, MOBA_BLOCK), F32),
            pltpu.SemaphoreType.DMA((KV_SLOTS, PAGES_PER_BLOCK)),
            pltpu.VMEM((N_PAST_BLOCKS, n_rows, MOBA_BLOCK), F32),
            pltpu.VMEM((N_PAST_BLOCKS, n_rows, MOBA_BLOCK), BF16),
        ],
    )
    return pl.pallas_call(
        _moba_sample_kernel,
        grid_spec=grid_spec,
        out_shape=jax.ShapeDtypeStruct((db, DEC_SEQ, ATT_WIDTH), F32),
        compiler_params=_params(1, 40),
        name="moba_sample",
    )(page_table, q, k_new, v_new, cache_kt, cache_vt)


def _mix_out_rows(pool, att, w_ref, g, res):
    y = _dot(pool.astype(BF16), w_ref[0:POOL_WIDTH, :])
    y = y + _dot(att.astype(BF16), w_ref[POOL_WIDTH:, :])
    return res + _rmsnorm(y, g)


def _mix_out_kernel(a_ref, b_ref, w_ref, g_ref, r_ref, o_ref):
    o_ref[...] = _mix_out_rows(a_ref[...], b_ref[...], w_ref, g_ref[...], r_ref[...])


def _mix_out(pool, att, w_bf, g, res):
    m = res.shape[0]
    tm = min(ROW_TILE, m)
    half_spec = pl.BlockSpec((tm, POOL_WIDTH), lambda i: (i, 0))
    row_spec = pl.BlockSpec((tm, D_MODEL), lambda i: (i, 0))
    return pl.pallas_call(
        _mix_out_kernel,
        grid=(m // tm,),
        in_specs=[half_spec, half_spec,
                  pl.BlockSpec((D_MODEL, D_MODEL), lambda i: (0, 0)),
                  pl.BlockSpec((1, D_MODEL), lambda i: (0, 0)),
                  row_spec],
        out_specs=row_spec,
        out_shape=jax.ShapeDtypeStruct((m, D_MODEL), F32),
        compiler_params=_params(1, 32),
        name="mix_out",
    )(pool, att, w_bf, g, res)


def _mem_kv_kernel(x_ref, g_ref, wk_ref, wv_ref, k_ref, v_ref):
    n = _rmsnorm(x_ref[...], g_ref[...]).astype(BF16)
    k_ref[...] = _dot(n, wk_ref[...])
    v_ref[...] = _dot(n, wv_ref[...])


def _mem_kv(mem, g, wk_bf, wv_bf):
    m = mem.shape[0]
    tm = MEM_LEN
    row_spec = pl.BlockSpec((tm, D_MODEL), lambda i: (i, 0))
    w_spec = pl.BlockSpec((D_MODEL, D_MODEL), lambda i: (0, 0))
    out = jax.ShapeDtypeStruct((m, D_MODEL), F32)
    return pl.pallas_call(
        _mem_kv_kernel,
        grid=(m // tm,),
        in_specs=[row_spec, pl.BlockSpec((1, D_MODEL), lambda i: (0, 0)), w_spec, w_spec],
        out_specs=[row_spec, row_spec],
        out_shape=[out, out],
        compiler_params=_params(1, 32),
        name="mem_kv",
    )(mem, g, wk_bf, wv_bf)


def _xattn_rows(h, gpre, wq_ref, mk_ref, mv_ref, wo_ref, gpost):
    scale = MEM_HEAD_DIM ** -0.5
    q = _dot(_rmsnorm(h, gpre).astype(BF16), wq_ref[...]).astype(BF16)
    y = jnp.zeros(h.shape, F32)
    for hh in range(MEM_HEADS):
        cols = slice(hh * MEM_HEAD_DIM, (hh + 1) * MEM_HEAD_DIM)
        s = _dot_nt(q[:, cols], mk_ref[0, :, cols].astype(BF16)) * scale
        p = jnp.exp(s - jnp.max(s, axis=-1, keepdims=True))
        p = (p / jnp.sum(p, axis=-1, keepdims=True)).astype(BF16)
        oh = _dot(p, mv_ref[0, :, cols].astype(BF16))
        y = y + _dot(oh.astype(BF16), wo_ref[cols, :])
    return h + _rmsnorm(y, gpost)

def _xattn_sample_kernel(h_ref, gpre_ref, wq_ref, mk_ref, mv_ref, wo_ref, gpost_ref, o_ref):
    tm = h_ref.shape[0]
    n_mem = mk_ref.shape[0]
    t_rows = tm // n_mem
    width = MEM_LEN * MEM_ROWS
    n_q = MEM_HEADS * t_rows
    scale = MEM_HEAD_DIM ** -0.5
    h = h_ref[...]
    q = _dot(_rmsnorm(h, gpre_ref[...]).astype(BF16), wq_ref[...]).astype(BF16)
    col = lax.broadcasted_iota(jnp.int32, (n_q, width), 1)
    row_head = lax.broadcasted_iota(jnp.int32, (n_q, width), 0) // t_rows
    own = (lax.rem(col, MEM_ROWS) == row_head)
    rows = []
    for b in range(n_mem):
        keys = mk_ref[b].reshape(width, LANES).astype(BF16)
        vals = mv_ref[b].reshape(width, LANES).astype(BF16)
        qb = q[b * t_rows:(b + 1) * t_rows, :]
        q_all = jnp.concatenate([qb[:, hh * MEM_HEAD_DIM + c * LANES:hh * MEM_HEAD_DIM + (c + 1) * LANES]
                                 for c in range(MEM_CHUNKS) for hh in range(MEM_HEADS)], axis=0)
        s_all = _dot_nt(q_all, keys)
        s = s_all[0:n_q]
        for c in range(1, MEM_CHUNKS):
            s = s + pltpu.roll(s_all[c * n_q:(c + 1) * n_q], width - c * MEM_HEADS, 1)
        s = jnp.where(own, s * scale, NEG_INF)
        p = jnp.exp(s - jnp.max(s, axis=-1, keepdims=True))
        p = p / jnp.sum(p, axis=-1, keepdims=True)
        p_all = jnp.concatenate([p] + [pltpu.roll(p, c * MEM_HEADS, 1) for c in range(1, MEM_CHUNKS)], axis=0)
        o_all = _dot(p_all.astype(BF16), vals)
        rows.append(jnp.concatenate(
            [o_all[(c * MEM_HEADS + hh) * t_rows:(c * MEM_HEADS + hh + 1) * t_rows, :]
             for hh in range(MEM_HEADS) for c in range(MEM_CHUNKS)], axis=1))
    o = jnp.concatenate(rows, axis=0)
    o_ref[...] = h + _rmsnorm(_dot(o.astype(BF16), wo_ref[...]), gpost_ref[...])


def _xattn_sample(h, gpre, wq_bf, mk, mv, wo_bf, gpost, n_mem):
    m = h.shape[0]
    tm = n_mem * (m // mk.shape[0])
    row_spec = pl.BlockSpec((tm, D_MODEL), lambda i: (i, 0))
    vec_spec = pl.BlockSpec((1, D_MODEL), lambda i: (0, 0))
    w_spec = pl.BlockSpec((D_MODEL, D_MODEL), lambda i: (0, 0))
    mem_spec = pl.BlockSpec((n_mem, MEM_LEN, MEM_ROWS, LANES), lambda i: (i, 0, 0, 0))
    return pl.pallas_call(
        _xattn_sample_kernel,
        grid=(m // tm,),
        in_specs=[row_spec, vec_spec, w_spec, mem_spec, mem_spec, w_spec, vec_spec],
        out_specs=row_spec,
        out_shape=jax.ShapeDtypeStruct((m, D_MODEL), F32),
        compiler_params=_params(1, 48),
        name="xattn_sample",
    )(h, gpre, wq_bf, mk, mv, wo_bf, gpost)


def _ffn_rows(h, gpre, wu_ref, wd_ref, gpost):
    n = _rmsnorm(h, gpre).astype(BF16)
    y = jnp.zeros(h.shape, F32)
    chunk = D_MODEL
    for c in range(D_FF // chunk):
        a = jnp.square(jnp.maximum(_dot(n, wu_ref[:, c * chunk:(c + 1) * chunk]), 0.0))
        y = y + _dot(a.astype(BF16), wd_ref[c * chunk:(c + 1) * chunk, :])
    return h + _rmsnorm(y, gpost)


def _ffn_kernel(h_ref, gpre_ref, wu_ref, wd_ref, gpost_ref, o_ref):
    o_ref[...] = _ffn_rows(h_ref[...], gpre_ref[...], wu_ref, wd_ref, gpost_ref[...])


def _post_prompt_kernel(pool_ref, att_ref, x_ref, wout_ref, gmix_ref, gprex_ref, wq_ref, mk_ref, mv_ref, wo_ref,
                        gpostx_ref, gpreffn_ref, wu_ref, wd_ref, gpostffn_ref, o_ref):
    h = _mix_out_rows(pool_ref[...], att_ref[0].T, wout_ref, gmix_ref[...], x_ref[...])
    h = _xattn_rows(h, gprex_ref[...], wq_ref, mk_ref, mv_ref, wo_ref, gpostx_ref[...])
    o_ref[...] = _ffn_rows(h, gpreffn_ref[...], wu_ref, wd_ref, gpostffn_ref[...])


def _post_prompt(pool, att_t, x, w_out_bf, g_mix, g_pre_x, wq_bf, mk, mv, wo_bf, g_post_x,
                 g_pre_ffn, wu_bf, wd_bf, g_post_ffn):
    m = x.shape[0]
    tm = ROW_TILE
    tiles = att_t.shape[2] // tm

    def resident(shape):
        return pl.BlockSpec(shape, lambda i: (0,) * len(shape), pipeline_mode=pl.Buffered(1))

    row_spec = pl.BlockSpec((tm, D_MODEL), lambda i: (i, 0))
    vec_spec = resident((1, D_MODEL))
    sq_spec = resident((D_MODEL, D_MODEL))
    mem_spec = pl.BlockSpec((1, MEM_LEN, D_MODEL), lambda i: (i // tiles, 0, 0))
    return pl.pallas_call(
        _post_prompt_kernel,
        grid=(m // tm,),
        in_specs=[pl.BlockSpec((tm, POOL_WIDTH), lambda i: (i, 0)),
                  pl.BlockSpec((1, ATT_WIDTH, tm), lambda i: (i // tiles, 0, i % tiles)),
                  row_spec, sq_spec, vec_spec, vec_spec, sq_spec, mem_spec, mem_spec, sq_spec, vec_spec,
                  vec_spec, resident((D_MODEL, D_FF)), resident((D_FF, D_MODEL)), vec_spec],
        out_specs=row_spec,
        out_shape=jax.ShapeDtypeStruct((m, D_MODEL), F32),
        compiler_params=_params(1, 58),
        name="post_prompt",
    )(pool, att_t, x, w_out_bf, g_mix, g_pre_x, wq_bf, mk, mv, wo_bf, g_post_x, g_pre_ffn, wu_bf, wd_bf, g_post_ffn)


def _ffn(h, gpre, wu_bf, wd_bf, gpost):
    m = h.shape[0]
    tm = min(ROW_TILE, m)
    row_spec = pl.BlockSpec((tm, D_MODEL), lambda i: (i, 0))
    vec_spec = pl.BlockSpec((1, D_MODEL), lambda i: (0, 0))
    return pl.pallas_call(
        _ffn_kernel,
        grid=(m // tm,),
        in_specs=[row_spec, vec_spec,
                  pl.BlockSpec((D_MODEL, D_FF), lambda i: (0, 0)),
                  pl.BlockSpec((D_FF, D_MODEL), lambda i: (0, 0)),
                  vec_spec],
        out_specs=row_spec,
        out_shape=jax.ShapeDtypeStruct((m, D_MODEL), F32),
        compiler_params=_params(1, 56),
        name="ffn",
    )(h, gpre, wu_bf, wd_bf, gpost)


def _rope_inv():
    inv = ROPE_THETA ** (-2.0 * jnp.arange(ROT_HALF, dtype=F32) / ROT_DIM)
    d = np.arange(LANES) % HEAD_DIM
    rotary = (d < ROT_DIM).astype(np.float32)
    inv_lanes = (inv[d % ROT_HALF] * rotary).reshape(1, LANES)
    return inv_lanes, inv.reshape(ROT_HALF, 1)


def kernel(x_prompt, x_sample, mem_prompt, cache_k, cache_v, cache_mem_k, cache_mem_v, state_pool, page_table,
           g_pre_mix, w_in, w_pool, pool_scale, w_out, g_post_mix,
           g_mem, g_pre_x, w_xq, w_xk, w_xv, w_xo, g_post_x,
           g_pre_ffn, w_up, w_down, g_post_ffn):
    depth = w_in.shape[0]
    batch = x_prompt.shape[0]
    db = x_sample.shape[0]
    n_phys = cache_k.shape[1]
    inv_lanes, inv_col = _rope_inv()

    hp = x_prompt.reshape(batch * SEQ, D_MODEL)
    hs = x_sample.reshape(db * DEC_SEQ, D_MODEL)
    pool_p, k_p, v_p, mk_p, mv_p, pool_s, k_s, v_s = ([] for _ in range(8))
    for l in range(depth):
        vec = lambda a: a[l].reshape(1, -1)
        w_in_bf = w_in[l].astype(BF16)
        w_u_bf = w_in_bf[:, :POOL_WIDTH]
        w_qkvt_bf = w_in_bf[:, POOL_WIDTH:].T
        w_pool_bf = w_pool[l].astype(BF16)
        w_out_bf = w_out[l].astype(BF16)
        w_xq_bf, w_xo_bf = w_xq[l].astype(BF16), w_xo[l].astype(BF16)
        w_up_bf, w_down_bf = w_up[l].astype(BF16), w_down[l].astype(BF16)

        pool, u_tail, qt, kt, vt = _in_proj_prompt(hp, vec(g_pre_mix), w_u_bf, w_qkvt_bf, inv_col,
                                                   w_pool_bf, vec(pool_scale), batch)
        mk, mv = _mem_kv(mem_prompt.reshape(batch * MEM_LEN, D_MODEL), vec(g_mem),
                         w_xk[l].astype(BF16), w_xv[l].astype(BF16))
        hp = _post_prompt(pool, _moba_prompt(qt, kt, vt), hp, w_out_bf, vec(g_post_mix),
                          vec(g_pre_x), w_xq_bf, mk.reshape(batch, MEM_LEN, D_MODEL),
                          mv.reshape(batch, MEM_LEN, D_MODEL), w_xo_bf, vec(g_post_x),
                          vec(g_pre_ffn), w_up_bf, w_down_bf, vec(g_post_ffn))
        pool_p.append(u_tail[:, POOL_HALO - POOL_STATE:])
        k_p.append(kt.reshape(batch, N_HEADS, HEAD_DIM, SEQ).transpose(0, 3, 1, 2))
        v_p.append(vt.reshape(batch, N_HEADS, HEAD_DIM, SEQ).transpose(0, 3, 1, 2))
        mk_p.append(mk.reshape(batch, MEM_LEN, MEM_HEADS, MEM_HEAD_DIM))
        mv_p.append(mv.reshape(batch, MEM_LEN, MEM_HEADS, MEM_HEAD_DIM))

        u, q, k, v = _in_proj_rows(hs, vec(g_pre_mix), w_in_bf, inv_lanes, DEC_SEQ, PAST_LEN)
        u_ext = jnp.concatenate([state_pool[l], u.reshape(db, DEC_SEQ, POOL_WIDTH)], axis=1)
        pool = _pool_sample(jnp.pad(u_ext, ((0, 0), (1, 0), (0, 0))), w_pool_bf, vec(pool_scale))
        cache_kt = cache_k[l].transpose(0, 2, 3, 1).reshape(n_phys, ATT_WIDTH, PAGE_SIZE)
        cache_vt = cache_v[l].transpose(0, 2, 3, 1).reshape(n_phys, ATT_WIDTH, PAGE_SIZE)
        att = _moba_sample(q.reshape(db, DEC_SEQ, ATT_WIDTH), k.reshape(db, DEC_SEQ, ATT_WIDTH),
                           v.reshape(db, DEC_SEQ, ATT_WIDTH), cache_kt, cache_vt, page_table)
        hs = _mix_out(pool, att.reshape(db * DEC_SEQ, ATT_WIDTH), w_out_bf, vec(g_post_mix), hs)
        mem_native = lambda a: a[l].reshape(db, MEM_LEN, MEM_HEADS, MEM_CHUNKS, LANES).transpose(
            0, 1, 3, 2, 4).reshape(db, MEM_LEN, MEM_ROWS, LANES)
        hs = _xattn_sample(hs, vec(g_pre_x), w_xq_bf, mem_native(cache_mem_k), mem_native(cache_mem_v),
                           w_xo_bf, vec(g_post_x), n_mem=4)
        hs = _ffn(hs, vec(g_pre_ffn), w_up_bf, w_down_bf, vec(g_post_ffn))
        pool_s.append(u_ext[:, u_ext.shape[1] - POOL_STATE:])
        k_s.append(k.reshape(db, DEC_SEQ, N_HEADS, HEAD_DIM))
        v_s.append(v.reshape(db, DEC_SEQ, N_HEADS, HEAD_DIM))

    return (hp.reshape(batch, SEQ, D_MODEL), hs.reshape(db, DEC_SEQ, D_MODEL),
            jnp.stack(pool_p), jnp.stack(k_p), jnp.stack(v_p), jnp.stack(mk_p), jnp.stack(mv_p),
            jnp.stack(pool_s), jnp.stack(k_s), jnp.stack(v_s))
```
